```python
import jax, jax.numpy as jnp
from jax import lax
import numpy as np

D_MODEL = 1024
BATCH = 8
SEQ = 4096
DEPTH = 4
DEC_BATCH = 32
DEC_SEQ = 64
PAST_LEN = 4096

CHUNK = 64
N_MIXERS = 2
N_CONV = len([i for i in range(DEPTH) if i % N_MIXERS == 0])
N_MLSTM = len([i for i in range(DEPTH) if i % N_MIXERS == 1])
CONV_WIDTH = D_MODEL
CONV_K = 31
M_WIDTH = 2 * D_MODEL
M_HEADS = 4
M_HEAD_DIM = M_WIDTH // M_HEADS
M_CONV_K = 4
EPS = 1e-6

kernel_name = 'hybrid_conv_mlstm_stream_step'


def rms_norm(x, g):
    xf = x.astype(jnp.float32)
    y = xf * lax.rsqrt(jnp.mean(xf * xf, axis=-1, keepdims=True) + EPS)
    return (y * g.astype(jnp.float32)).astype(x.dtype)


def layer_norm(x, g, b):
    xf = x.astype(jnp.float32)
    mu = jnp.mean(xf, axis=-1, keepdims=True)
    var = jnp.mean(jnp.square(xf - mu), axis=-1, keepdims=True)
    y = (xf - mu) * lax.rsqrt(var + EPS)
    return (y * g.astype(jnp.float32) + b.astype(jnp.float32)).astype(x.dtype)


def head_norm(h, g):
    mu = jnp.mean(h, axis=-1, keepdims=True)
    var = jnp.mean(jnp.square(h - mu), axis=-1, keepdims=True)
    return (h - mu) * lax.rsqrt(var + EPS) * g.astype(jnp.float32)


def causal_dwconv(x, buf, w, b):
    xp = jnp.concatenate([buf.astype(x.dtype), x], axis=1)
    y = lax.conv_general_dilated(xp, w[:, None, :].astype(x.dtype), window_strides=(1,),
                                 padding='VALID', dimension_numbers=('NWC', 'WIO', 'NWC'),
                                 feature_group_count=x.shape[-1])
    return y + b, xp[:, xp.shape[1] - (w.shape[0] - 1):]


def conv_branch(h, buf, w_in, w_dw, b_dw, ln_g, ln_b, w_out):
    u = jnp.einsum('btd,de->bte', h, w_in)
    a, a_gate, z = jnp.split(u, 3, axis=-1)
    g = a * jax.nn.sigmoid(a_gate)
    y, new_buf = causal_dwconv(g, buf, w_dw, b_dw)
    y = layer_norm(y, ln_g, ln_b)
    y = jax.nn.silu(y) * jax.nn.silu(z)
    return jnp.einsum('bte,ed->btd', y, w_out), new_buf


def mlstm_chunk(carry, inp):
    C, n, m = carry
    q, k, v, ig, lf = inp
    q = q.astype(jnp.float32)
    k = k.astype(jnp.float32) * (M_HEAD_DIM ** -0.5)
    v = v.astype(jnp.float32)
    L = q.shape[1]
    b = jnp.cumsum(lf, axis=1).transpose(0, 2, 1)
    igt = ig.transpose(0, 2, 1)
    a = b + m[..., None]
    causal = jnp.tril(jnp.ones((L, L), dtype=bool))
    logw = jnp.where(causal, b[..., :, None] - b[..., None, :] + igt[..., None, :], -jnp.inf)
    mt = jnp.maximum(a, jnp.max(logw, axis=-1))
    w_inter = jnp.exp(a - mt)
    w_intra = jnp.exp(logw - mt[..., None])
    s = jnp.einsum('bthd,bshd->bhts', q, k) * w_intra
    num = w_inter[..., None] * jnp.einsum('bhvd,bthd->bhtv', C, q) + jnp.einsum('bhts,bshv->bhtv', s, v)
    den = w_inter * jnp.einsum('bhd,bthd->bht', n, q) + jnp.sum(s, axis=-1)
    den = jnp.maximum(jnp.abs(den), jnp.exp(-mt))
    hh = (num / den[..., None]).transpose(0, 2, 1, 3)
    m_new = mt[..., -1]
    decay = jnp.exp(b[..., -1] + m - m_new)
    wk = jnp.exp(b[..., -1:] - b + igt - m_new[..., None])
    C_new = decay[..., None, None] * C + jnp.einsum('bhs,bshv,bshd->bhvd', wk, v, k)
    n_new = decay[..., None] * n + jnp.einsum('bhs,bshd->bhd', wk, k)
    return (C_new, n_new, m_new), hh


def mlstm_recurrence(q, k, v, ig, lf, C, n, m):
    B, T, H, DH = q.shape
    L = CHUNK if T % CHUNK == 0 else T
    nc = T // L

    def to_chunks(t):
        return jnp.moveaxis(t.reshape((B, nc, L) + t.shape[2:]), 1, 0)

    (C, n, m), hs = lax.scan(mlstm_chunk, (C, n, m),
                             (to_chunks(q), to_chunks(k), to_chunks(v), to_chunks(ig), to_chunks(lf)))
    return jnp.moveaxis(hs, 0, 1).reshape(B, T, H, DH), C, n, m


def mlstm_branch(h, conv_buf, C, n, m, w_in, w_conv, b_conv, w_q, w_k, w_v, w_gate, b_gate,
                 gn_g, skip, w_out):
    B, T, _ = h.shape
    u = jnp.einsum('btd,de->bte', h, w_in)
    xm, z, o = jnp.split(u, 3, axis=-1)
    xc, new_buf = causal_dwconv(xm, conv_buf, w_conv, b_conv)
    xc = jax.nn.silu(xc)
    xch = xc.reshape(B, T, M_HEADS, M_HEAD_DIM)
    xmh = xm.reshape(B, T, M_HEADS, M_HEAD_DIM)
    q = jnp.einsum('bthd,hde->bthe', xch, w_q)
    k = jnp.einsum('bthd,hde->bthe', xch, w_k)
    v = jnp.einsum('bthd,hde->bthe', xmh, w_v)
    gpre = (jnp.einsum('bthd,hdg->btg', q, w_gate[0]) + jnp.einsum('bthd,hdg->btg', k, w_gate[1])
            + jnp.einsum('bthd,hdg->btg', v, w_gate[2]) + b_gate).astype(jnp.float32)
    ig, fg = jnp.split(gpre, 2, axis=-1)
    lf = jax.nn.log_sigmoid(fg)
    hh, C, n, m = mlstm_recurrence(q, k, v, ig, lf, C, n, m)
    hh = jax.nn.sigmoid(o.astype(jnp.float32)).reshape(B, T, M_HEADS, M_HEAD_DIM) * hh
    hn = head_norm(hh, gn_g.reshape(M_HEADS, M_HEAD_DIM)).reshape(B, T, M_WIDTH).astype(h.dtype)
    y = (hn + skip * xc) * jax.nn.silu(z)
    return jnp.einsum('bte,ed->btd', y, w_out), new_buf, C, n, m


def run_trunk(x, c, conv_buf, mconv_buf, C0, n0, m0, P):
    new_conv, new_mconv, new_C, new_n, new_m = [], [], [], [], []
    for i in range(DEPTH):
        mod = jnp.einsum('bd,de->be', c, P['ada_w'][i]) + P['ada_b'][i]
        shift, scale, gate = jnp.split(mod[:, None, :], 3, axis=-1)
        h = rms_norm(x, P['norm_g'][i]) * (1 + scale) + shift
        j = i // N_MIXERS
        if i % N_MIXERS == 0:
            out, nb = conv_branch(h, conv_buf[j], P['cv_w_in'][j], P['cv_w_dw'][j], P['cv_b_dw'][j],
                                  P['cv_ln_g'][j], P['cv_ln_b'][j], P['cv_w_out'][j])
            new_conv.append(nb)
        else:
            out, nb, C, n, m = mlstm_branch(
                h, mconv_buf[j], C0[j].astype(jnp.float32), n0[j].astype(jnp.float32),
                m0[j].astype(jnp.float32), P['ml_w_in'][j], P['ml_w_conv'][j], P['ml_b_conv'][j],
                P['ml_w_q'][j], P['ml_w_k'][j], P['ml_w_v'][j], P['ml_w_gate'][j], P['ml_b_gate'][j],
                P['ml_gn_g'][j], P['ml_skip'][j], P['ml_w_out'][j])
            new_mconv.append(nb)
            new_C.append(C)
            new_n.append(n)
            new_m.append(m)
        x = x + gate * out
    y = rms_norm(x, P['final_g'])
    return y, jnp.stack(new_conv), jnp.stack(new_mconv), jnp.stack(new_C), jnp.stack(new_n), jnp.stack(new_m)


def setup_inputs(seed: int = 0) -> dict:
    key = jax.random.key(seed)
    ks = iter(jax.random.split(key, 40))
    nrm = lambda shape, s: jax.random.normal(next(ks), shape, jnp.float32) * s
    H, DH = M_HEADS, M_HEAD_DIM
    b_gate_i = nrm((N_MLSTM, H), 0.1)
    b_gate_f = jnp.linspace(3.0, 6.0, H, dtype=jnp.float32)[None, :] + nrm((N_MLSTM, H), 0.1)
    return {
        'x_prompt': nrm((BATCH, SEQ, D_MODEL), 1.0),
        'x_sample': nrm((DEC_BATCH, DEC_SEQ, D_MODEL), 1.0),
        'c_prompt': nrm((BATCH, D_MODEL), 1.0),
        'c_sample': nrm((DEC_BATCH, D_MODEL), 1.0),
        'state_conv': nrm((N_CONV, DEC_BATCH, CONV_K - 1, CONV_WIDTH), 0.5),
        'state_mconv': nrm((N_MLSTM, DEC_BATCH, M_CONV_K - 1, M_WIDTH), 0.5),
        'state_C': nrm((N_MLSTM, DEC_BATCH, H, DH, DH), 0.1),
        'state_n': nrm((N_MLSTM, DEC_BATCH, H, DH), 0.1),
        'state_m': nrm((N_MLSTM, DEC_BATCH, H), 0.5),
        'norm_g': 1.0 + nrm((DEPTH, D_MODEL), 0.01),
        'ada_w': nrm((DEPTH, D_MODEL, 3 * D_MODEL), 0.2 * D_MODEL ** -0.5),
        'ada_b': nrm((DEPTH, 3 * D_MODEL), 0.02),
        'cv_w_in': nrm((N_CONV, D_MODEL, 3 * CONV_WIDTH), D_MODEL ** -0.5),
        'cv_w_dw': nrm((N_CONV, CONV_K, CONV_WIDTH), CONV_K ** -0.5),
        'cv_b_dw': nrm((N_CONV, CONV_WIDTH), 0.01),
        'cv_ln_g': 1.0 + nrm((N_CONV, CONV_WIDTH), 0.01),
        'cv_ln_b': nrm((N_CONV, CONV_WIDTH), 0.01),
        'cv_w_out': nrm((N_CONV, CONV_WIDTH, D_MODEL), CONV_WIDTH ** -0.5),
        'ml_w_in': nrm((N_MLSTM, D_MODEL, 3 * M_WIDTH), D_MODEL ** -0.5),
        'ml_w_conv': nrm((N_MLSTM, M_CONV_K, M_WIDTH), M_CONV_K ** -0.5),
        'ml_b_conv': nrm((N_MLSTM, M_WIDTH), 0.01),
        'ml_w_q': nrm((N_MLSTM, H, DH, DH), DH ** -0.5),
        'ml_w_k': nrm((N_MLSTM, H, DH, DH), DH ** -0.5),
        'ml_w_v': nrm((N_MLSTM, H, DH, DH), DH ** -0.5),
        'ml_w_gate': nrm((N_MLSTM, 3, H, DH, 2 * H), (3 * M_WIDTH) ** -0.5),
        'ml_b_gate': jnp.concatenate([b_gate_i, b_gate_f], axis=-1),
        'ml_gn_g': 1.0 + nrm((N_MLSTM, M_WIDTH), 0.01),
        'ml_skip': 1.0 + nrm((N_MLSTM, M_WIDTH), 0.01),
        'ml_w_out': nrm((N_MLSTM, M_WIDTH, D_MODEL), M_WIDTH ** -0.5),
        'final_g': 1.0 + nrm((D_MODEL,), 0.01),
    }


def reference(x_prompt, x_sample, c_prompt, c_sample, state_conv, state_mconv, state_C, state_n, state_m,
              norm_g, ada_w, ada_b, cv_w_in, cv_w_dw, cv_b_dw, cv_ln_g, cv_ln_b, cv_w_out,
              ml_w_in, ml_w_conv, ml_b_conv, ml_w_q, ml_w_k, ml_w_v, ml_w_gate, ml_b_gate,
              ml_gn_g, ml_skip, ml_w_out, final_g):
    P = dict(norm_g=norm_g, ada_w=ada_w, ada_b=ada_b, cv_w_in=cv_w_in, cv_w_dw=cv_w_dw, cv_b_dw=cv_b_dw,
             cv_ln_g=cv_ln_g, cv_ln_b=cv_ln_b, cv_w_out=cv_w_out, ml_w_in=ml_w_in, ml_w_conv=ml_w_conv,
             ml_b_conv=ml_b_conv, ml_w_q=ml_w_q, ml_w_k=ml_w_k, ml_w_v=ml_w_v, ml_w_gate=ml_w_gate,
             ml_b_gate=ml_b_gate, ml_gn_g=ml_gn_g, ml_skip=ml_skip, ml_w_out=ml_w_out, final_g=final_g)
    Bp = x_prompt.shape[0]
    z_conv = jnp.zeros((N_CONV, Bp, CONV_K - 1, CONV_WIDTH), x_prompt.dtype)
    z_mconv = jnp.zeros((N_MLSTM, Bp, M_CONV_K - 1, M_WIDTH), x_prompt.dtype)
    z_C = jnp.zeros((N_MLSTM, Bp, M_HEADS, M_HEAD_DIM, M_HEAD_DIM), jnp.float32)
    z_n = jnp.zeros((N_MLSTM, Bp, M_HEADS, M_HEAD_DIM), jnp.float32)
    z_m = jnp.zeros((N_MLSTM, Bp, M_HEADS), jnp.float32)
    y_prompt, p_conv, p_mconv, p_C, p_n, p_m = run_trunk(x_prompt, c_prompt, z_conv, z_mconv, z_C, z_n, z_m, P)
    y_sample, s_conv, s_mconv, s_C, s_n, s_m = run_trunk(x_sample, c_sample, state_conv, state_mconv,
                                                         state_C, state_n, state_m, P)
    return (y_prompt, y_sample, p_conv, p_mconv, p_C, p_n, p_m, s_conv, s_mconv, s_C, s_n, s_m)
```

```python
import functools

import jax
import jax.numpy as jnp
from jax import lax
from jax.experimental import pallas as pl
from jax.experimental.pallas import tpu as pltpu

EPS = 1e-6
CONV_K = 31
M_CONV_K = 4
M_HEADS = 4
GATE_PAD = 128
CONV_HALO = 32
MCONV_HALO = 8
VMEM_LIMIT_BYTES = 58 * 1024 * 1024

F32 = jnp.float32
BF16 = jnp.bfloat16
HIGHEST = lax.Precision.HIGHEST


def _const_spec(shape):
    nd = len(shape)
    return pl.BlockSpec(shape, lambda *_: (0,) * nd, pipeline_mode=pl.Buffered(1))


def _sigmoid(x):
    return 1.0 / (1.0 + jnp.exp(-x))


def _silu(x):
    return x * _sigmoid(x)


def _log_sigmoid(x):
    return jnp.minimum(x, 0.0) - jnp.log1p(jnp.exp(-jnp.abs(x)))


def _modulated_rms_norm(x, g, shift, scale):
    y = x * lax.rsqrt(jnp.mean(x * x, axis=-1, keepdims=True) + EPS)
    return (y * g) * (1.0 + scale) + shift


def _ada_kernel(c_ref, w_ref, b_ref, o_ref):
    o_ref[0] = jnp.dot(c_ref[...], w_ref[0], preferred_element_type=F32, precision=HIGHEST) + b_ref[0]


def _ada_call(c_all, ada_w, ada_b):
    depth, d, e = ada_w.shape
    nb = c_all.shape[0]
    tn = 1024
    return pl.pallas_call(
        _ada_kernel,
        out_shape=jax.ShapeDtypeStruct((depth, nb, e), F32),
        grid=(depth, e // tn),
        in_specs=[
            pl.BlockSpec((nb, d), lambda i, j: (0, 0)),
            pl.BlockSpec((1, d, tn), lambda i, j: (i, 0, j)),
            pl.BlockSpec((1, 1, tn), lambda i, j: (i, 0, j)),
        ],
        out_specs=pl.BlockSpec((1, nb, tn), lambda i, j: (i, 0, j)),
        compiler_params=pltpu.CompilerParams(dimension_semantics=("arbitrary", "arbitrary")),
        name="ada_mod",
    )(c_all, ada_w, ada_b.reshape(depth, 1, e))


def _conv_layer_kernel(x_ref, mod_ref, buf_ref, ng_ref, win_ref, wdw_ref, bdw_ref, lng_ref, lnb_ref, wout_ref,
                       xo_ref, bufo_ref, gbuf, ybuf, *, bb, tt, d, c, row_blk, lane_blk):
    t = pl.program_id(1)
    nt = pl.num_programs(1)
    rows = bb * tt

    @pl.when(t == 0)
    def _():
        gbuf[:, CONV_HALO - (CONV_K - 1):CONV_HALO, :] = buf_ref[...]

    x = x_ref[...].reshape(rows, d)
    mod = mod_ref[...]
    shift = mod[:, :, 0:d]
    scale = mod[:, :, d:2 * d]
    gate = mod[:, :, 2 * d:3 * d]
    h = _modulated_rms_norm(x_ref[...], ng_ref[...], shift, scale).reshape(rows, d)
    u = jnp.dot(h.astype(BF16), win_ref[...], preferred_element_type=F32)
    g = u[:, 0:c] * _sigmoid(u[:, c:2 * c])
    z = u[:, 2 * c:3 * c]
    gbuf[:, CONV_HALO:CONV_HALO + tt, :] = g.reshape(bb, tt, c)

    base = CONV_HALO - (CONV_K - 1)
    for r0 in range(0, tt, row_blk):
        for c0 in range(0, c, lane_blk):
            acc = jnp.broadcast_to(bdw_ref[:, c0:c0 + lane_blk], (bb, row_blk, lane_blk))
            for k in range(CONV_K):
                acc = acc + wdw_ref[k:k + 1, c0:c0 + lane_blk] * gbuf[:, base + r0 + k:base + r0 + k + row_blk,
                                                                      c0:c0 + lane_blk]
            ybuf[:, r0:r0 + row_blk, c0:c0 + lane_blk] = acc

    y = ybuf[...].reshape(rows, c)
    mu = jnp.mean(y, axis=-1, keepdims=True)
    yc = y - mu
    var = jnp.mean(yc * yc, axis=-1, keepdims=True)
    y = yc * lax.rsqrt(var + EPS) * lng_ref[...] + lnb_ref[...]
    y = _silu(y) * _silu(z)
    out = jnp.dot(y.astype(BF16), wout_ref[...], preferred_element_type=F32)
    xo_ref[...] = x_ref[...] + gate * out.reshape(bb, tt, d)

    @pl.when(t == nt - 1)
    def _():
        bufo_ref[...] = gbuf[:, tt + CONV_HALO - (CONV_K - 1):tt + CONV_HALO, :]

    @pl.when(t < nt - 1)
    def _():
        gbuf[:, 0:CONV_HALO, :] = gbuf[:, tt:tt + CONV_HALO, :]


def _conv_layer(x, mod, buf, norm_g, w_in, w_dw, b_dw, ln_g, ln_b, w_out, *, bb, tt):
    b, t, d = x.shape
    c = w_dw.shape[1]
    assert b % bb == 0 and t % tt == 0 and tt % 8 == 0 and tt >= CONV_HALO
    row_blk = 64 if tt % 64 == 0 else tt
    kern = functools.partial(_conv_layer_kernel, bb=bb, tt=tt, d=d, c=c, row_blk=row_blk, lane_blk=256)
    return pl.pallas_call(
        kern,
        out_shape=(jax.ShapeDtypeStruct((b, t, d), F32), jax.ShapeDtypeStruct((b, CONV_K - 1, c), F32)),
        grid=(b // bb, t // tt),
        in_specs=[
            pl.BlockSpec((bb, tt, d), lambda i, j: (i, j, 0)),
            pl.BlockSpec((bb, 1, 3 * d), lambda i, j: (i, 0, 0)),
            pl.BlockSpec((bb, CONV_K - 1, c), lambda i, j: (i, 0, 0)),
            _const_spec((1, d)),
            _const_spec((d, 3 * c)),
            _const_spec((CONV_K, c)),
            _const_spec((1, c)),
            _const_spec((1, c)),
            _const_spec((1, c)),
            _const_spec((c, d)),
        ],
        out_specs=(
            pl.BlockSpec((bb, tt, d), lambda i, j: (i, j, 0)),
            pl.BlockSpec((bb, CONV_K - 1, c), lambda i, j: (i, 0, 0)),
        ),
        scratch_shapes=[
            pltpu.VMEM((bb, CONV_HALO + tt, c), F32),
            pltpu.VMEM((bb, tt, c), F32),
        ],
        compiler_params=pltpu.CompilerParams(dimension_semantics=("arbitrary", "arbitrary"),
                                             vmem_limit_bytes=VMEM_LIMIT_BYTES),
        name="conv_layer",
    )(x, mod.reshape(b, 1, 3 * d), buf, norm_g.reshape(1, d), w_in, w_dw, b_dw.reshape(1, c),
      ln_g.reshape(1, c), ln_b.reshape(1, c), w_out)


def _mlstm_layer_kernel(x_ref, mod_ref, mbuf_ref, c0_ref, n0_ref, m0_ref,
                        ng_ref, win_ref, wc_ref, bc_ref, wq_ref, wk_ref, wv_ref, wg_ref, bg_ref,
                        gng_ref, skip_ref, wout_ref,
                        xo_ref, mbufo_ref, co_ref, no_ref, mo_ref,
                        xbuf, q_s, k_s, v_s, hh_s, *, ll, d, w, dh):
    t = pl.program_id(1)
    nt = pl.num_programs(1)
    nh = M_HEADS

    @pl.when(t == 0)
    def _():
        xbuf[MCONV_HALO - (M_CONV_K - 1):MCONV_HALO, :] = mbuf_ref[0]
        co_ref[...] = c0_ref[...]
        no_ref[...] = n0_ref[...]
        mo_ref[...] = m0_ref[...]

    x = x_ref[0]
    mod = mod_ref[0]
    shift = mod[:, 0:d]
    scale = mod[:, d:2 * d]
    gate = mod[:, 2 * d:3 * d]
    h = _modulated_rms_norm(x, ng_ref[...], shift, scale)
    u = jnp.dot(h.astype(BF16), win_ref[...], preferred_element_type=F32)
    xm = u[:, 0:w]
    z = u[:, w:2 * w]
    o = u[:, 2 * w:3 * w]

    xbuf[MCONV_HALO:MCONV_HALO + ll, :] = xm
    base = MCONV_HALO - (M_CONV_K - 1)
    xc = jnp.broadcast_to(bc_ref[...], (ll, w))
    for k in range(M_CONV_K):
        xc = xc + wc_ref[k:k + 1, :] * xbuf[base + k:base + k + ll, :]
    xc = _silu(xc)

    @pl.when(t == nt - 1)
    def _():
        mbufo_ref[0] = xbuf[ll + MCONV_HALO - (M_CONV_K - 1):ll + MCONV_HALO, :]

    @pl.when(t < nt - 1)
    def _():
        xbuf[0:MCONV_HALO, :] = xbuf[ll:ll + MCONV_HALO, :]

    xc_b = xc.astype(BF16)
    xm_b = xm.astype(BF16)
    for hd in range(nh):
        sl = slice(hd * dh, (hd + 1) * dh)
        q_s[:, sl] = jnp.dot(xc_b[:, sl], wq_ref[hd], preferred_element_type=F32)
        k_s[:, sl] = jnp.dot(xc_b[:, sl], wk_ref[hd], preferred_element_type=F32)
        v_s[:, sl] = jnp.dot(xm_b[:, sl], wv_ref[hd], preferred_element_type=F32)

    gpre = (jnp.dot(q_s[...].astype(BF16), wg_ref[0], preferred_element_type=F32)
            + jnp.dot(k_s[...].astype(BF16), wg_ref[1], preferred_element_type=F32)
            + jnp.dot(v_s[...].astype(BF16), wg_ref[2], preferred_element_type=F32)
            + bg_ref[...])
    lf = _log_sigmoid(gpre)
    row = lax.broadcasted_iota(jnp.int32, (ll, ll), 0)
    col = lax.broadcasted_iota(jnp.int32, (ll, ll), 1)
    causal = col <= row
    tril = jnp.where(causal, 1.0, 0.0).astype(F32)
    bcum = jnp.dot(tril, lf, preferred_element_type=F32, precision=HIGHEST)
    ident = jnp.where(lax.broadcasted_iota(jnp.int32, (GATE_PAD, GATE_PAD), 0)
                      == lax.broadcasted_iota(jnp.int32, (GATE_PAD, GATE_PAD), 1), 1.0, 0.0).astype(F32)
    nt_dims = (((1,), (1,)), ((), ()))
    gpre_t = lax.dot_general(ident, gpre, nt_dims, preferred_element_type=F32, precision=HIGHEST)
    bcum_t = lax.dot_general(ident, bcum, nt_dims, preferred_element_type=F32, precision=HIGHEST)

    kscale = dh ** -0.5
    for hd in range(nh):
        sl = slice(hd * dh, (hd + 1) * dh)
        ig_c = gpre[:, hd:hd + 1]
        b_c = bcum[:, nh + hd:nh + hd + 1]
        ig_r = gpre_t[hd:hd + 1, :]
        b_r = bcum_t[nh + hd:nh + hd + 1, :]
        m_prev = mo_ref[0, :, hd:hd + 1]
        c_prev = co_ref[0, hd]
        n_prev = no_ref[0, hd:hd + 1, :]

        qf = q_s[:, sl]
        kf = k_s[:, sl] * kscale
        vf = v_s[:, sl]
        qb = qf.astype(BF16)
        kb = kf.astype(BF16)
        vb = vf.astype(BF16)

        a = b_c + m_prev
        logw = jnp.where(causal, b_c - b_r + ig_r, -jnp.inf)
        mt = jnp.maximum(a, jnp.max(logw, axis=-1, keepdims=True))
        w_inter = jnp.exp(a - mt)
        w_intra = jnp.exp(logw - mt)
        s = lax.dot_general(qb, kb, nt_dims, preferred_element_type=F32) * w_intra
        inter = lax.dot_general(qb, c_prev.astype(BF16), nt_dims, preferred_element_type=F32)
        num = w_inter * inter + jnp.dot(s.astype(BF16), vb, preferred_element_type=F32)
        qn = jnp.sum(qb.astype(F32) * n_prev, axis=-1, keepdims=True)
        den = w_inter * qn + jnp.sum(s, axis=-1, keepdims=True)
        den = jnp.maximum(jnp.abs(den), jnp.exp(-mt))
        hh_s[:, sl] = num / den

        m_new = mt[ll - 1:ll, :]
        b_last = b_c[ll - 1:ll, :]
        decay = jnp.exp(b_last + m_prev - m_new)
        wk_c = jnp.exp(b_last - b_c + ig_c - m_new)
        vw = (vf * wk_c).astype(BF16)
        upd = lax.dot_general(vw, kb, (((0,), (0,)), ((), ())), preferred_element_type=F32)
        co_ref[0, hd] = decay * c_prev + upd
        no_ref[0, hd:hd + 1, :] = decay * n_prev + jnp.sum(kb.astype(F32) * wk_c, axis=0, keepdims=True)
        mo_ref[0, :, hd:hd + 1] = m_new

    og = _sigmoid(o)
    zs = _silu(z)
    for hd in range(nh):
        sl = slice(hd * dh, (hd + 1) * dh)
        hv = og[:, sl] * hh_s[:, sl]
        mu = jnp.mean(hv, axis=-1, keepdims=True)
        hc = hv - mu
        var = jnp.mean(hc * hc, axis=-1, keepdims=True)
        hn = hc * lax.rsqrt(var + EPS) * gng_ref[:, sl]
        hh_s[:, sl] = (hn + skip_ref[:, sl] * xc[:, sl]) * zs[:, sl]
    out = jnp.dot(hh_s[...].astype(BF16), wout_ref[...], preferred_element_type=F32)
    xo_ref[0] = x + gate * out


def _mlstm_layer(x, mod, mbuf, c0, n0, m0, norm_g, w_in, w_conv, b_conv, w_q, w_k, w_v, w_gate_pad, b_gate_pad,
                 gn_g, skip, w_out, *, ll):
    b, t, d = x.shape
    w = w_conv.shape[1]
    nh = M_HEADS
    dh = w // nh
    assert t % ll == 0 and ll % 8 == 0 and ll >= MCONV_HALO
    kern = functools.partial(_mlstm_layer_kernel, ll=ll, d=d, w=w, dh=dh)
    return pl.pallas_call(
        kern,
        out_shape=(
            jax.ShapeDtypeStruct((b, t, d), F32),
            jax.ShapeDtypeStruct((b, M_CONV_K - 1, w), F32),
            jax.ShapeDtypeStruct((b, nh, dh, dh), F32),
            jax.ShapeDtypeStruct((b, nh, dh), F32),
            jax.ShapeDtypeStruct((b, 1, nh), F32),
        ),
        grid=(b, t // ll),
        in_specs=[
            pl.BlockSpec((1, ll, d), lambda i, j: (i, j, 0)),
            pl.BlockSpec((1, 1, 3 * d), lambda i, j: (i, 0, 0)),
            pl.BlockSpec((1, M_CONV_K - 1, w), lambda i, j: (i, 0, 0)),
            pl.BlockSpec((1, nh, dh, dh), lambda i, j: (i, 0, 0, 0)),
            pl.BlockSpec((1, nh, dh), lambda i, j: (i, 0, 0)),
            pl.BlockSpec((1, 1, nh), lambda i, j: (i, 0, 0)),
            _const_spec((1, d)),
            _const_spec((d, 3 * w)),
            _const_spec((M_CONV_K, w)),
            _const_spec((1, w)),
            _const_spec((nh, dh, dh)),
            _const_spec((nh, dh, dh)),
            _const_spec((nh, dh, dh)),
            _const_spec((3, w, GATE_PAD)),
            _const_spec((1, GATE_PAD)),
            _const_spec((1, w)),
            _const_spec((1, w)),
            _const_spec((w, d)),
        ],
        out_specs=(
            pl.BlockSpec((1, ll, d), lambda i, j: (i, j, 0)),
            pl.BlockSpec((1, M_CONV_K - 1, w), lambda i, j: (i, 0, 0)),
            pl.BlockSpec((1, nh, dh, dh), lambda i, j: (i, 0, 0, 0)),
            pl.BlockSpec((1, nh, dh), lambda i, j: (i, 0, 0)),
            pl.BlockSpec((1, 1, nh), lambda i, j: (i, 0, 0)),
        ),
        scratch_shapes=[
            pltpu.VMEM((MCONV_HALO + ll, w), F32),
            pltpu.VMEM((ll, w), F32),
            pltpu.VMEM((ll, w), F32),
            pltpu.VMEM((ll, w), F32),
            pltpu.VMEM((ll, w), F32),
        ],
        compiler_params=pltpu.CompilerParams(dimension_semantics=("arbitrary", "arbitrary"),
                                             vmem_limit_bytes=VMEM_LIMIT_BYTES),
        name="mlstm_layer",
    )(x, mod.reshape(b, 1, 3 * d), mbuf, c0, n0, m0.reshape(b, 1, nh), norm_g.reshape(1, d), w_in, w_conv,
      b_conv.reshape(1, w), w_q, w_k, w_v, w_gate_pad, b_gate_pad, gn_g.reshape(1, w), skip.reshape(1, w), w_out)


def _final_norm_kernel(x_ref, g_ref, o_ref):
    x = x_ref[...]
    o_ref[...] = x * lax.rsqrt(jnp.mean(x * x, axis=-1, keepdims=True) + EPS) * g_ref[...]


def _final_norm(x, g, *, rows_blk=1024):
    b, t, d = x.shape
    rows = b * t
    rows_blk = min(rows_blk, rows)
    y = pl.pallas_call(
        _final_norm_kernel,
        out_shape=jax.ShapeDtypeStruct((rows, d), F32),
        grid=(rows // rows_blk,),
        in_specs=[pl.BlockSpec((rows_blk, d), lambda i: (i, 0)), pl.BlockSpec((1, d), lambda i: (0, 0))],
        out_specs=pl.BlockSpec((rows_blk, d), lambda i: (i, 0)),
        compiler_params=pltpu.CompilerParams(dimension_semantics=("arbitrary",)),
        name="final_norm",
    )(x.reshape(rows, d), g.reshape(1, d))
    return y.reshape(b, t, d)


def _prep_weights(P):
    n_ml, _, nh, dh, ng = P['ml_w_gate'].shape
    wg = P['ml_w_gate'].reshape(n_ml, 3, nh * dh, ng)
    wg = jnp.pad(wg, ((0, 0), (0, 0), (0, 0), (0, GATE_PAD - ng))).astype(BF16)
    bg = jnp.pad(P['ml_b_gate'], ((0, 0), (0, GATE_PAD - ng))).reshape(n_ml, 1, GATE_PAD)
    return dict(
        cv_w_in=P['cv_w_in'].astype(BF16), cv_w_out=P['cv_w_out'].astype(BF16),
        ml_w_in=P['ml_w_in'].astype(BF16), ml_w_q=P['ml_w_q'].astype(BF16), ml_w_k=P['ml_w_k'].astype(BF16),
        ml_w_v=P['ml_w_v'].astype(BF16), ml_w_out=P['ml_w_out'].astype(BF16), ml_w_gate=wg, ml_b_gate=bg)


def _run_trunk(x, mods, conv_buf, mconv_buf, c0, n0, m0, P, W, *, conv_bb, conv_tt, ml_ll):
    depth = P['norm_g'].shape[0]
    new_conv, new_mconv, new_c, new_n, new_m = [], [], [], [], []
    for i in range(depth):
        j = i // 2
        if i % 2 == 0:
            x, nb = _conv_layer(x, mods[i], conv_buf[j], P['norm_g'][i], W['cv_w_in'][j], P['cv_w_dw'][j],
                                P['cv_b_dw'][j], P['cv_ln_g'][j], P['cv_ln_b'][j], W['cv_w_out'][j],
                                bb=conv_bb, tt=conv_tt)
            new_conv.append(nb)
        else:
            x, nb, c, n, m = _mlstm_layer(x, mods[i], mconv_buf[j], c0[j], n0[j], m0[j], P['norm_g'][i],
                                          W['ml_w_in'][j], P['ml_w_conv'][j], P['ml_b_conv'][j], W['ml_w_q'][j],
                                          W['ml_w_k'][j], W['ml_w_v'][j], W['ml_w_gate'][j], W['ml_b_gate'][j],
                                          P['ml_gn_g'][j], P['ml_skip'][j], W['ml_w_out'][j], ll=ml_ll)
            new_mconv.append(nb)
            new_c.append(c)
            new_n.append(n)
            new_m.append(m.reshape(m.shape[0], m.shape[2]))
    y = _final_norm(x, P['final_g'])
    return y, jnp.stack(new_conv), jnp.stack(new_mconv), jnp.stack(new_c), jnp.stack(new_n), jnp.stack(new_m)


def _pick_rows(b, t, target):
    if t >= target:
        return 1, target
    bb = max(1, min(b, target // t))
    while b % bb:
        bb -= 1
    return bb, t


def kernel(x_prompt, x_sample, c_prompt, c_sample, state_conv, state_mconv, state_C, state_n, state_m, norm_g, ada_w, ada_b, cv_w_in, cv_w_dw, cv_b_dw, cv_ln_g, cv_ln_b, cv_w_out, ml_w_in, ml_w_conv, ml_b_conv, ml_w_q, ml_w_k, ml_w_v, ml_w_gate, ml_b_gate, ml_gn_g, ml_skip, ml_w_out, final_g):
    P = dict(norm_g=norm_g, cv_w_in=cv_w_in, cv_w_dw=cv_w_dw, cv_b_dw=cv_b_dw, cv_ln_g=cv_ln_g, cv_ln_b=cv_ln_b,
             cv_w_out=cv_w_out, ml_w_in=ml_w_in, ml_w_conv=ml_w_conv, ml_b_conv=ml_b_conv, ml_w_q=ml_w_q,
             ml_w_k=ml_w_k, ml_w_v=ml_w_v, ml_w_gate=ml_w_gate, ml_b_gate=ml_b_gate, ml_gn_g=ml_gn_g,
             ml_skip=ml_skip, ml_w_out=ml_w_out, final_g=final_g)
    W = _prep_weights(P)
    bp, tp, _ = x_prompt.shape
    bs, ts, _ = x_sample.shape
    n_conv, _, kc, cw = state_conv.shape
    n_ml, _, km, mw = state_mconv.shape
    nh, dh = state_C.shape[2], state_C.shape[3]

    mods = _ada_call(jnp.concatenate([c_prompt, c_sample], axis=0), ada_w, ada_b)
    mods_p = mods[:, :bp]
    mods_s = mods[:, bp:]

    z_conv = jnp.zeros((n_conv, bp, kc, cw), F32)
    z_mconv = jnp.zeros((n_ml, bp, km, mw), F32)
    z_c = jnp.zeros((n_ml, bp, nh, dh, dh), F32)
    z_n = jnp.zeros((n_ml, bp, nh, dh), F32)
    z_m = jnp.zeros((n_ml, bp, nh), F32)

    bb_p, tt_p = _pick_rows(bp, tp, 256)
    bb_s, tt_s = _pick_rows(bs, ts, 256)
    outs_p = _run_trunk(x_prompt, mods_p, z_conv, z_mconv, z_c, z_n, z_m, P, W,
                        conv_bb=bb_p, conv_tt=tt_p, ml_ll=min(tp, 128))
    outs_s = _run_trunk(x_sample, mods_s, state_conv, state_mconv, state_C, state_n, state_m, P, W,
                        conv_bb=bb_s, conv_tt=tt_s, ml_ll=min(ts, 128))
    return (outs_p[0], outs_s[0]) + tuple(outs_p[1:]) + tuple(outs_s[1:])
```

```python
import functools

import jax
import jax.numpy as jnp
from jax import lax
from jax.experimental import pallas as pl
from jax.experimental.pallas import tpu as pltpu

EPS = 1e-6
CONV_K = 31
M_CONV_K = 4
M_HEADS = 4
GATE_PAD = 128
CONV_HALO = 32
MCONV_HALO = 8
VMEM_LIMIT_BYTES = 58 * 1024 * 1024

F32 = jnp.float32
BF16 = jnp.bfloat16
HIGHEST = lax.Precision.HIGHEST


def _const_spec(shape):
    nd = len(shape)
    return pl.BlockSpec(shape, lambda *_: (0,) * nd, pipeline_mode=pl.Buffered(1))


def _sigmoid(x):
    return 1.0 / (1.0 + jnp.exp(-x))


def _silu(x):
    return x * _sigmoid(x)


def _log_sigmoid(x):
    return jnp.minimum(x, 0.0) - jnp.log1p(jnp.exp(-jnp.abs(x)))


def _modulated_rms_norm(x, g, shift, scale):
    y = x * lax.rsqrt(jnp.mean(x * x, axis=-1, keepdims=True) + EPS)
    return (y * g) * (1.0 + scale) + shift


def _ada_kernel(c_ref, w_ref, b_ref, o_ref):
    o_ref[0] = jnp.dot(c_ref[...], w_ref[0], preferred_element_type=F32, precision=HIGHEST) + b_ref[0]


def _ada_call(c_all, ada_w, ada_b):
    depth, d, e = ada_w.shape
    nb = c_all.shape[0]
    tn = 1024
    return pl.pallas_call(
        _ada_kernel,
        out_shape=jax.ShapeDtypeStruct((depth, nb, e), F32),
        grid=(depth, e // tn),
        in_specs=[
            pl.BlockSpec((nb, d), lambda i, j: (0, 0)),
            pl.BlockSpec((1, d, tn), lambda i, j: (i, 0, j)),
            pl.BlockSpec((1, 1, tn), lambda i, j: (i, 0, j)),
        ],
        out_specs=pl.BlockSpec((1, nb, tn), lambda i, j: (i, 0, j)),
        compiler_params=pltpu.CompilerParams(dimension_semantics=("arbitrary", "arbitrary")),
        name="ada_mod",
    )(c_all, ada_w, ada_b.reshape(depth, 1, e))


def _conv_layer_kernel(*refs, bb, tt, d, c, row_blk, lane_blk, has_state):
    it = iter(refs)
    x_ref, mod_ref = next(it), next(it)
    if has_state:
        buf_ref = next(it)
    ng_ref, win_ref, wdw_ref, bdw_ref, lng_ref, lnb_ref, wout_ref = [next(it) for _ in range(7)]
    xo_ref, bufo_ref, gbuf, ybuf = [next(it) for _ in range(4)]

    t = pl.program_id(1)
    nt = pl.num_programs(1)
    rows = bb * tt

    @pl.when(t == 0)
    def _():
        if has_state:
            gbuf[:, CONV_HALO - (CONV_K - 1):CONV_HALO, :] = buf_ref[0]
        else:
            gbuf[:, 0:CONV_HALO, :] = jnp.zeros((bb, CONV_HALO, c), F32)

    mod = mod_ref[...]
    shift = mod[:, :, 0:d]
    scale = mod[:, :, d:2 * d]
    gate = mod[:, :, 2 * d:3 * d]
    h = _modulated_rms_norm(x_ref[...], ng_ref[...], shift, scale).reshape(rows, d)
    u = jnp.dot(h.astype(BF16), win_ref[...], preferred_element_type=F32)
    g = u[:, 0:c] * _sigmoid(u[:, c:2 * c])
    z = u[:, 2 * c:3 * c]
    gbuf[:, CONV_HALO:CONV_HALO + tt, :] = g.reshape(bb, tt, c)

    base = CONV_HALO - (CONV_K - 1)
    for r0 in range(0, tt, row_blk):
        for c0 in range(0, c, lane_blk):
            acc = jnp.broadcast_to(bdw_ref[:, c0:c0 + lane_blk], (bb, row_blk, lane_blk))
            for k in range(CONV_K):
                acc = acc + wdw_ref[k:k + 1, c0:c0 + lane_blk] * gbuf[:, base + r0 + k:base + r0 + k + row_blk,
                                                                      c0:c0 + lane_blk]
            ybuf[:, r0:r0 + row_blk, c0:c0 + lane_blk] = acc

    y = ybuf[...].reshape(rows, c)
    mu = jnp.mean(y, axis=-1, keepdims=True)
    yc = y - mu
    var = jnp.mean(yc * yc, axis=-1, keepdims=True)
    y = yc * lax.rsqrt(var + EPS) * lng_ref[...] + lnb_ref[...]
    y = _silu(y) * _silu(z)
    out = jnp.dot(y.astype(BF16), wout_ref[...], preferred_element_type=F32)
    xo_ref[...] = x_ref[...] + gate * out.reshape(bb, tt, d)

    @pl.when(t == nt - 1)
    def _():
        bufo_ref[...] = gbuf[:, tt + CONV_HALO - (CONV_K - 1):tt + CONV_HALO, :]

    @pl.when(t < nt - 1)
    def _():
        gbuf[:, 0:CONV_HALO, :] = gbuf[:, tt:tt + CONV_HALO, :]


def _conv_layer(x, mod, state, layer, norm_g, w_in, w_dw, b_dw, ln_g, ln_b, w_out, *, bb, tt):
    b, t, d = x.shape
    c = w_dw.shape[1]
    assert b % bb == 0 and t % tt == 0 and tt % 8 == 0 and tt >= CONV_HALO
    has_state = state is not None
    row_blk = 64 if tt % 64 == 0 else tt
    kern = functools.partial(_conv_layer_kernel, bb=bb, tt=tt, d=d, c=c, row_blk=row_blk, lane_blk=256,
                             has_state=has_state)
    inputs = [x, mod.reshape(b, 1, 3 * d)]
    in_specs = [
        pl.BlockSpec((bb, tt, d), lambda i, j: (i, j, 0)),
        pl.BlockSpec((bb, 1, 3 * d), lambda i, j: (i, 0, 0)),
    ]
    if has_state:
        inputs.append(state)
        in_specs.append(pl.BlockSpec((1, bb, CONV_K - 1, c), lambda i, j: (layer, i, 0, 0)))
    inputs += [norm_g.reshape(1, d), w_in, w_dw, b_dw.reshape(1, c), ln_g.reshape(1, c), ln_b.reshape(1, c), w_out]
    in_specs += [
        _const_spec((1, d)),
        _const_spec((d, 3 * c)),
        _const_spec((CONV_K, c)),
        _const_spec((1, c)),
        _const_spec((1, c)),
        _const_spec((1, c)),
        _const_spec((c, d)),
    ]
    return pl.pallas_call(
        kern,
        out_shape=(jax.ShapeDtypeStruct((b, t, d), F32), jax.ShapeDtypeStruct((b, CONV_K - 1, c), F32)),
        grid=(b // bb, t // tt),
        in_specs=in_specs,
        out_specs=(
            pl.BlockSpec((bb, tt, d), lambda i, j: (i, j, 0)),
            pl.BlockSpec((bb, CONV_K - 1, c), lambda i, j: (i, 0, 0)),
        ),
        scratch_shapes=[
            pltpu.VMEM((bb, CONV_HALO + tt, c), F32),
            pltpu.VMEM((bb, tt, c), F32),
        ],
        compiler_params=pltpu.CompilerParams(dimension_semantics=("arbitrary", "arbitrary"),
                                             vmem_limit_bytes=VMEM_LIMIT_BYTES),
        name="conv_layer",
    )(*inputs)


def _mlstm_layer_kernel(*refs, ll, d, w, dh, has_state, aliased):
    it = iter(refs)
    x_ref, mod_ref = next(it), next(it)
    if has_state:
        mbuf_ref, c0_ref, n0_ref, m0_ref = next(it), next(it), next(it), next(it)
    (ng_ref, win_ref, wc_ref, bc_ref, wq_ref, wk_ref, wv_ref, wg_ref, bg_ref, gng_ref, skip_ref,
     wout_ref) = [next(it) for _ in range(12)]
    if aliased:
        next(it)
    xo_ref, mbufo_ref, co_ref, no_ref, mo_ref = [next(it) for _ in range(5)]
    xbuf, xc_s, q_s, k_s, v_s, zs_s, og_s, hh_s = [next(it) for _ in range(8)]

    t = pl.program_id(1)
    nt = pl.num_programs(1)
    nh = M_HEADS

    @pl.when(t == 0)
    def _():
        if has_state:
            xbuf[MCONV_HALO - (M_CONV_K - 1):MCONV_HALO, :] = mbuf_ref[0, 0]
            co_ref[...] = c0_ref[...]
            no_ref[...] = n0_ref[0]
            mo_ref[...] = m0_ref[0]
        else:
            xbuf[0:MCONV_HALO, :] = jnp.zeros((MCONV_HALO, w), F32)
            co_ref[...] = jnp.zeros(co_ref.shape, F32)
            no_ref[...] = jnp.zeros(no_ref.shape, F32)
            mo_ref[...] = jnp.zeros(mo_ref.shape, F32)

    x = x_ref[0]
    mod = mod_ref[0]
    shift = mod[:, 0:d]
    scale = mod[:, d:2 * d]
    gate = mod[:, 2 * d:3 * d]
    hb = _modulated_rms_norm(x, ng_ref[...], shift, scale).astype(BF16)
    xm = jnp.dot(hb, win_ref[:, 0:w], preferred_element_type=F32)
    zs_s[...] = _silu(jnp.dot(hb, win_ref[:, w:2 * w], preferred_element_type=F32)).astype(BF16)
    og_s[...] = _sigmoid(jnp.dot(hb, win_ref[:, 2 * w:3 * w], preferred_element_type=F32)).astype(BF16)

    xbuf[MCONV_HALO:MCONV_HALO + ll, :] = xm
    base = MCONV_HALO - (M_CONV_K - 1)
    xc = jnp.broadcast_to(bc_ref[...], (ll, w))
    for k in range(M_CONV_K):
        xc = xc + wc_ref[k:k + 1, :] * xbuf[base + k:base + k + ll, :]
    xc = _silu(xc)
    xc_s[...] = xc

    @pl.when(t == nt - 1)
    def _():
        mbufo_ref[0] = xbuf[ll + MCONV_HALO - (M_CONV_K - 1):ll + MCONV_HALO, :]

    @pl.when(t < nt - 1)
    def _():
        xbuf[0:MCONV_HALO, :] = xbuf[ll:ll + MCONV_HALO, :]

    xc_b = xc.astype(BF16)
    xm_b = xm.astype(BF16)
    for hd in range(nh):
        sl = slice(hd * dh, (hd + 1) * dh)
        q_s[:, sl] = jnp.dot(xc_b[:, sl], wq_ref[hd], preferred_element_type=F32).astype(BF16)
        k_s[:, sl] = jnp.dot(xc_b[:, sl], wk_ref[hd], preferred_element_type=F32).astype(BF16)
        v_s[:, sl] = jnp.dot(xm_b[:, sl], wv_ref[hd], preferred_element_type=F32).astype(BF16)

    gpre = (jnp.dot(q_s[...], wg_ref[0], preferred_element_type=F32)
            + jnp.dot(k_s[...], wg_ref[1], preferred_element_type=F32)
            + jnp.dot(v_s[...], wg_ref[2], preferred_element_type=F32)
            + bg_ref[...])
    lf = _log_sigmoid(gpre)
    row = lax.broadcasted_iota(jnp.int32, (ll, ll), 0)
    col = lax.broadcasted_iota(jnp.int32, (ll, ll), 1)
    causal = col <= row
    tril = jnp.where(causal, 1.0, 0.0).astype(F32)
    bcum = jnp.dot(tril, lf, preferred_element_type=F32, precision=HIGHEST)
    ident = jnp.where(lax.broadcasted_iota(jnp.int32, (GATE_PAD, GATE_PAD), 0)
                      == lax.broadcasted_iota(jnp.int32, (GATE_PAD, GATE_PAD), 1), 1.0, 0.0).astype(F32)
    nt_dims = (((1,), (1,)), ((), ()))
    gpre_t = lax.dot_general(ident, gpre, nt_dims, preferred_element_type=F32, precision=HIGHEST)
    bcum_t = lax.dot_general(ident, bcum, nt_dims, preferred_element_type=F32, precision=HIGHEST)

    kscale = dh ** -0.5
    for hd in range(nh):
        sl = slice(hd * dh, (hd + 1) * dh)
        ig_c = gpre[:, hd:hd + 1]
        b_c = bcum[:, nh + hd:nh + hd + 1]
        ig_r = gpre_t[hd:hd + 1, :]
        b_r = bcum_t[nh + hd:nh + hd + 1, :]
        m_prev = mo_ref[0, :, hd:hd + 1]
        c_prev = co_ref[0, 0, hd]
        n_prev = no_ref[0, hd:hd + 1, :]
        qb = q_s[:, sl]
        kb = k_s[:, sl]
        vb = v_s[:, sl]

        a = b_c + m_prev
        logw = jnp.where(causal, b_c - b_r + ig_r, -jnp.inf)
        mt = jnp.maximum(a, jnp.max(logw, axis=-1, keepdims=True))
        w_inter = jnp.exp(a - mt)
        w_intra = jnp.exp(logw - mt) * kscale
        s = lax.dot_general(qb, kb, nt_dims, preferred_element_type=F32) * w_intra
        inter = lax.dot_general(qb, c_prev.astype(BF16), nt_dims, preferred_element_type=F32)
        num = w_inter * inter + jnp.dot(s.astype(BF16), vb, preferred_element_type=F32)
        qn = jnp.sum(qb.astype(F32) * n_prev, axis=-1, keepdims=True)
        den = w_inter * qn + jnp.sum(s, axis=-1, keepdims=True)
        den = jnp.maximum(jnp.abs(den), jnp.exp(-mt))
        hh_s[:, sl] = num * (1.0 / den)

        m_new = mt[ll - 1:ll, :]
        b_last = b_c[ll - 1:ll, :]
        decay = jnp.exp(b_last + m_prev - m_new)
        wk_c = jnp.exp(b_last - b_c + ig_c - m_new) * kscale
        vw = (vb.astype(F32) * wk_c).astype(BF16)
        upd = lax.dot_general(vw, kb, (((0,), (0,)), ((), ())), preferred_element_type=F32)
        co_ref[0, 0, hd] = decay * c_prev + upd
        no_ref[0, hd:hd + 1, :] = decay * n_prev + jnp.sum(kb.astype(F32) * wk_c, axis=0, keepdims=True)
        mo_ref[0, :, hd:hd + 1] = m_new

    for hd in range(nh):
        sl = slice(hd * dh, (hd + 1) * dh)
        hv = og_s[:, sl].astype(F32) * hh_s[:, sl]
        mu = jnp.mean(hv, axis=-1, keepdims=True)
        hc = hv - mu
        var = jnp.mean(hc * hc, axis=-1, keepdims=True)
        hn = hc * lax.rsqrt(var + EPS) * gng_ref[:, sl]
        hh_s[:, sl] = (hn + skip_ref[:, sl] * xc_s[:, sl]) * zs_s[:, sl].astype(F32)
    out = jnp.dot(hh_s[...].astype(BF16), wout_ref[...], preferred_element_type=F32)
    xo_ref[0] = x + gate * out


def _mlstm_layer(x, mod, state, c_stack, layer, n_layers, norm_g, w_in, w_conv, b_conv, w_q, w_k, w_v, w_gate_pad,
                 b_gate_pad, gn_g, skip, w_out, *, ll):
    b, t, d = x.shape
    w = w_conv.shape[1]
    nh = M_HEADS
    dh = w // nh
    assert t % ll == 0 and ll % 8 == 0 and ll >= MCONV_HALO
    has_state = state is not None
    aliased = c_stack is not None
    kern = functools.partial(_mlstm_layer_kernel, ll=ll, d=d, w=w, dh=dh, has_state=has_state, aliased=aliased)
    inputs = [x, mod.reshape(b, 1, 3 * d)]
    in_specs = [
        pl.BlockSpec((1, ll, d), lambda i, j: (i, j, 0)),
        pl.BlockSpec((1, 1, 3 * d), lambda i, j: (i, 0, 0)),
    ]
    if has_state:
        mbuf, c0, n0, m0 = state
        inputs += [mbuf, c0, n0, m0.reshape(m0.shape[0], b, 1, nh)]
        in_specs += [
            pl.BlockSpec((1, 1, M_CONV_K - 1, w), lambda i, j: (layer, i, 0, 0)),
            pl.BlockSpec((1, 1, nh, dh, dh), lambda i, j: (layer, i, 0, 0, 0)),
            pl.BlockSpec((1, 1, nh, dh), lambda i, j: (layer, i, 0, 0)),
            pl.BlockSpec((1, 1, 1, nh), lambda i, j: (layer, i, 0, 0)),
        ]
    inputs += [norm_g.reshape(1, d), w_in, w_conv, b_conv.reshape(1, w), w_q, w_k, w_v, w_gate_pad, b_gate_pad,
               gn_g.reshape(1, w), skip.reshape(1, w), w_out]
    in_specs += [
        _const_spec((1, d)),
        _const_spec((d, 3 * w)),
        _const_spec((M_CONV_K, w)),
        _const_spec((1, w)),
        _const_spec((nh, dh, dh)),
        _const_spec((nh, dh, dh)),
        _const_spec((nh, dh, dh)),
        _const_spec((3, w, GATE_PAD)),
        _const_spec((1, GATE_PAD)),
        _const_spec((1, w)),
        _const_spec((1, w)),
        _const_spec((w, d)),
    ]
    aliases = {}
    if aliased:
        aliases = {len(inputs): 2}
        inputs.append(c_stack)
        in_specs.append(pl.BlockSpec(memory_space=pl.ANY))
    return pl.pallas_call(
        kern,
        out_shape=(
            jax.ShapeDtypeStruct((b, t, d), F32),
            jax.ShapeDtypeStruct((b, M_CONV_K - 1, w), F32),
            jax.ShapeDtypeStruct((n_layers, b, nh, dh, dh), F32),
            jax.ShapeDtypeStruct((b, nh, dh), F32),
            jax.ShapeDtypeStruct((b, 1, nh), F32),
        ),
        grid=(b, t // ll),
        in_specs=in_specs,
        out_specs=(
            pl.BlockSpec((1, ll, d), lambda i, j: (i, j, 0)),
            pl.BlockSpec((1, M_CONV_K - 1, w), lambda i, j: (i, 0, 0)),
            pl.BlockSpec((1, 1, nh, dh, dh), lambda i, j: (layer, i, 0, 0, 0)),
            pl.BlockSpec((1, nh, dh), lambda i, j: (i, 0, 0)),
            pl.BlockSpec((1, 1, nh), lambda i, j: (i, 0, 0)),
        ),
        scratch_shapes=[
            pltpu.VMEM((MCONV_HALO + ll, w), F32),
            pltpu.VMEM((ll, w), F32),
            pltpu.VMEM((ll, w), BF16),
            pltpu.VMEM((ll, w), BF16),
            pltpu.VMEM((ll, w), BF16),
            pltpu.VMEM((ll, w), BF16),
            pltpu.VMEM((ll, w), BF16),
            pltpu.VMEM((ll, w), F32),
        ],
        input_output_aliases=aliases,
        compiler_params=pltpu.CompilerParams(dimension_semantics=("arbitrary", "arbitrary"),
                                             vmem_limit_bytes=VMEM_LIMIT_BYTES),
        name="mlstm_layer",
    )(*inputs)


def _final_norm_kernel(x_ref, g_ref, o_ref):
    x = x_ref[...]
    o_ref[...] = x * lax.rsqrt(jnp.mean(x * x, axis=-1, keepdims=True) + EPS) * g_ref[...]


def _final_norm(x, g, *, rows_blk=1024):
    b, t, d = x.shape
    rows = b * t
    rows_blk = min(rows_blk, rows)
    y = pl.pallas_call(
        _final_norm_kernel,
        out_shape=jax.ShapeDtypeStruct((rows, d), F32),
        grid=(rows // rows_blk,),
        in_specs=[pl.BlockSpec((rows_blk, d), lambda i: (i, 0)), pl.BlockSpec((1, d), lambda i: (0, 0))],
        out_specs=pl.BlockSpec((rows_blk, d), lambda i: (i, 0)),
        compiler_params=pltpu.CompilerParams(dimension_semantics=("arbitrary",)),
        name="final_norm",
    )(x.reshape(rows, d), g.reshape(1, d))
    return y.reshape(b, t, d)


def _prep_weights(P):
    n_ml, _, nh, dh, ng = P['ml_w_gate'].shape
    wg = P['ml_w_gate'].reshape(n_ml, 3, nh * dh, ng)
    wg = jnp.pad(wg, ((0, 0), (0, 0), (0, 0), (0, GATE_PAD - ng))).astype(BF16)
    bg = jnp.pad(P['ml_b_gate'], ((0, 0), (0, GATE_PAD - ng))).reshape(n_ml, 1, GATE_PAD)
    return dict(
        cv_w_in=P['cv_w_in'].astype(BF16), cv_w_out=P['cv_w_out'].astype(BF16),
        ml_w_in=P['ml_w_in'].astype(BF16), ml_w_q=P['ml_w_q'].astype(BF16), ml_w_k=P['ml_w_k'].astype(BF16),
        ml_w_v=P['ml_w_v'].astype(BF16), ml_w_out=P['ml_w_out'].astype(BF16), ml_w_gate=wg, ml_b_gate=bg)


def _run_trunk(x, mods, state, P, W, *, conv_bb, conv_tt, ml_ll):
    depth = P['norm_g'].shape[0]
    n_ml = P['ml_w_in'].shape[0]
    new_conv, new_mconv, new_n, new_m = [], [], [], []
    c_stack = None
    for i in range(depth):
        j = i // 2
        if i % 2 == 0:
            x, nb = _conv_layer(x, mods[i], None if state is None else state[0], j, P['norm_g'][i], W['cv_w_in'][j],
                                P['cv_w_dw'][j], P['cv_b_dw'][j], P['cv_ln_g'][j], P['cv_ln_b'][j], W['cv_w_out'][j],
                                bb=conv_bb, tt=conv_tt)
            new_conv.append(nb)
        else:
            x, nb, c_stack, n, m = _mlstm_layer(
                x, mods[i], None if state is None else state[1:], c_stack, j, n_ml, P['norm_g'][i], W['ml_w_in'][j],
                P['ml_w_conv'][j], P['ml_b_conv'][j], W['ml_w_q'][j], W['ml_w_k'][j], W['ml_w_v'][j],
                W['ml_w_gate'][j], W['ml_b_gate'][j], P['ml_gn_g'][j], P['ml_skip'][j], W['ml_w_out'][j], ll=ml_ll)
            new_mconv.append(nb)
            new_n.append(n)
            new_m.append(m.reshape(m.shape[0], m.shape[2]))
    y = _final_norm(x, P['final_g'])
    return y, jnp.stack(new_conv), jnp.stack(new_mconv), c_stack, jnp.stack(new_n), jnp.stack(new_m)


def _pick_rows(b, t, target):
    if t >= target:
        return 1, target
    bb = max(1, min(b, target // t))
    while b % bb:
        bb -= 1
    return bb, t


def kernel(x_prompt, x_sample, c_prompt, c_sample, state_conv, state_mconv, state_C, state_n, state_m, norm_g, ada_w, ada_b, cv_w_in, cv_w_dw, cv_b_dw, cv_ln_g, cv_ln_b, cv_w_out, ml_w_in, ml_w_conv, ml_b_conv, ml_w_q, ml_w_k, ml_w_v, ml_w_gate, ml_b_gate, ml_gn_g, ml_skip, ml_w_out, final_g):
    P = dict(norm_g=norm_g, cv_w_in=cv_w_in, cv_w_dw=cv_w_dw, cv_b_dw=cv_b_dw, cv_ln_g=cv_ln_g, cv_ln_b=cv_ln_b,
             cv_w_out=cv_w_out, ml_w_in=ml_w_in, ml_w_conv=ml_w_conv, ml_b_conv=ml_b_conv, ml_w_q=ml_w_q,
             ml_w_k=ml_w_k, ml_w_v=ml_w_v, ml_w_gate=ml_w_gate, ml_b_gate=ml_b_gate, ml_gn_g=ml_gn_g,
             ml_skip=ml_skip, ml_w_out=ml_w_out, final_g=final_g)
    W = _prep_weights(P)
    bp, tp, _ = x_prompt.shape
    bs, ts, _ = x_sample.shape

    mods = _ada_call(jnp.concatenate([c_prompt, c_sample], axis=0), ada_w, ada_b)
    mods_p = mods[:, :bp]
    mods_s = mods[:, bp:]

    bb_p, tt_p = _pick_rows(bp, tp, 256)
    bb_s, tt_s = _pick_rows(bs, ts, 256)
    outs_p = _run_trunk(x_prompt, mods_p, None, P, W, conv_bb=bb_p, conv_tt=tt_p, ml_ll=min(tp, 256))
    outs_s = _run_trunk(x_sample, mods_s, (state_conv, state_mconv, state_C, state_n, state_m), P, W,
                        conv_bb=bb_s, conv_tt=tt_s, ml_ll=min(ts, 256))
    return (outs_p[0], outs_s[0]) + tuple(outs_p[1:]) + tuple(outs_s[1:])
```

```python
import functools

import jax
import jax.numpy as jnp
from jax import lax
from jax.experimental import pallas as pl
from jax.experimental.pallas import tpu as pltpu

EPS = 1e-6
CONV_K = 31
M_CONV_K = 4
M_HEADS = 4
GATE_PAD = 128
CONV_HALO = 32
MCONV_HALO = 8
NSTREAM = 8
LANES = 128
PROMPT_TILE = 256
VMEM_LIMIT_BYTES = 58 * 1024 * 1024

F32 = jnp.float32
BF16 = jnp.bfloat16
HIGHEST = lax.Precision.HIGHEST


def _const_spec(shape):
    nd = len(shape)
    return pl.BlockSpec(shape, lambda *_: (0,) * nd, pipeline_mode=pl.Buffered(1))


def _sigmoid(x):
    return 1.0 / (1.0 + jnp.exp(-x))


def _silu(x):
    return x * _sigmoid(x)


def _log_sigmoid(x):
    return jnp.minimum(x, 0.0) - jnp.log1p(jnp.exp(-jnp.abs(x)))


def _modulated_rms_norm(x, g, shift, scale):
    y = x * lax.rsqrt(jnp.mean(x * x, axis=-1, keepdims=True) + EPS)
    return (y * g) * (1.0 + scale) + shift


def _ada_kernel(c_ref, w_ref, b_ref, o_ref):
    o_ref[0] = jnp.dot(c_ref[...], w_ref[0], preferred_element_type=F32, precision=HIGHEST) + b_ref[0]


def _ada_call(c_all, ada_w, ada_b):
    depth, d, e = ada_w.shape
    nb = c_all.shape[0]
    tn = 1024
    return pl.pallas_call(
        _ada_kernel,
        out_shape=jax.ShapeDtypeStruct((depth, nb, e), F32),
        grid=(depth, e // tn),
        in_specs=[
            pl.BlockSpec((nb, d), lambda i, j: (0, 0)),
            pl.BlockSpec((1, d, tn), lambda i, j: (i, 0, j)),
            pl.BlockSpec((1, 1, tn), lambda i, j: (i, 0, j)),
        ],
        out_specs=pl.BlockSpec((1, nb, tn), lambda i, j: (i, 0, j)),
        compiler_params=pltpu.CompilerParams(dimension_semantics=("arbitrary", "arbitrary")),
        name="ada_mod",
    )(c_all, ada_w, ada_b.reshape(depth, 1, e))


def _conv_layer_kernel(*refs, ss, d, c, jb, grp, tail_chunks, chained, il_in, il_out):
    it = iter(refs)
    x_ref, mod_ref = next(it), next(it)
    if not chained:
        buf_ref = next(it)
    ng_ref, win_ref, wdw_ref, bdw_ref, lng_ref, lnb_ref, wout_ref = [next(it) for _ in range(7)]
    xo_ref, bufo_ref = next(it), next(it)
    xp_s, gbuf, ybuf, zs_s = next(it), next(it), next(it), next(it)
    if chained:
        prev_s = next(it)

    t = pl.program_id(1)
    nt = pl.num_programs(1)
    rows = ss * NSTREAM
    halo0 = CONV_HALO - (CONV_K - 1)

    nq = d // LANES
    if il_in:
        xp_s[...] = x_ref[0].reshape(ss, NSTREAM, d)
    else:
        for j in range(ss):
            for q in range(nq):
                xp_s[j, :, q * LANES:(q + 1) * LANES] = x_ref[
                    0, pl.ds((j // 8 * nq + q) * 8 + j % 8, NSTREAM, stride=ss * nq), :]
    x3 = xp_s[...]
    mod = mod_ref[...]
    shift = mod[:, :, 0:d]
    scale = mod[:, :, d:2 * d]
    gate = mod[:, :, 2 * d:3 * d]
    hb = _modulated_rms_norm(x3, ng_ref[...], shift, scale).reshape(rows, d).astype(BF16)

    if chained:
        @pl.when(t == 0)
        def _():
            prev_s[...] = jnp.zeros(prev_s.shape, F32)

    for g0 in range(0, c, grp):
        gs = slice(g0, g0 + grp)
        a = jnp.dot(hb, win_ref[:, g0:g0 + grp], preferred_element_type=F32)
        ag = jnp.dot(hb, win_ref[:, c + g0:c + g0 + grp], preferred_element_type=F32)
        gbuf[CONV_HALO:CONV_HALO + ss, :, gs] = (a * _sigmoid(ag)).reshape(ss, NSTREAM, grp)

        if chained:
            rolled = pltpu.roll(gbuf[ss:ss + CONV_HALO, :, gs], 1, axis=1)
            sub = lax.broadcasted_iota(jnp.int32, rolled.shape, 1)
            gbuf[0:CONV_HALO, :, gs] = jnp.where(sub == 0, prev_s[:, :, gs], rolled)
            prev_s[:, :, gs] = rolled
        else:
            gbuf[halo0:CONV_HALO, :, gs] = buf_ref[0, 0, :, :, gs]

        for c0 in range(g0, g0 + grp, LANES):
            cs = slice(c0, c0 + LANES)
            wb = [wdw_ref[k, :, cs] for k in range(CONV_K)]
            for j0 in range(0, ss, jb):
                acc = [bdw_ref[:, cs] for _ in range(jb)]
                for i in range(jb + CONV_K - 1):
                    xin = gbuf[halo0 + j0 + i, :, cs]
                    for jj in range(jb):
                        k = i - jj
                        if 0 <= k < CONV_K:
                            acc[jj] = acc[jj] + wb[k] * xin
                for jj in range(jb):
                    ybuf[j0 + jj, :, cs] = acc[jj]

    zs_s[...] = _silu(jnp.dot(hb, win_ref[:, 2 * c:3 * c], preferred_element_type=F32)).astype(BF16)

    jc = ss // tail_chunks
    for j0 in range(0, ss, jc):
        y = ybuf[j0:j0 + jc].reshape(jc * NSTREAM, c)
        mu = jnp.mean(y, axis=-1, keepdims=True)
        yc = y - mu
        var = jnp.mean(yc * yc, axis=-1, keepdims=True)
        y = yc * lax.rsqrt(var + EPS) * lng_ref[...] + lnb_ref[...]
        y = _silu(y) * zs_s[j0 * NSTREAM:(j0 + jc) * NSTREAM, :].astype(F32)
        out = jnp.dot(y.astype(BF16), wout_ref[...], preferred_element_type=F32)
        xnew = xp_s[j0:j0 + jc] + gate * out.reshape(jc, NSTREAM, d)
        if il_out:
            xo_ref[0, j0 * NSTREAM:(j0 + jc) * NSTREAM, :] = xnew.reshape(jc * NSTREAM, d)
        else:
            xp_s[j0:j0 + jc] = xnew
            for j in range(j0, j0 + jc):
                for q in range(nq):
                    xo_ref[0, pl.ds((j // 8 * nq + q) * 8 + j % 8, NSTREAM, stride=ss * nq), :] = xp_s[
                        j, :, q * LANES:(q + 1) * LANES]

    if chained:
        @pl.when(t == nt - 1)
        def _():
            bufo_ref[0] = prev_s[halo0:CONV_HALO]
    else:
        bufo_ref[0] = gbuf[ss + halo0:ss + CONV_HALO]


def _to_vreg_tiles(x):
    n, rows, d = x.shape
    nq = d // LANES
    return x.reshape(n, rows // 8, 8, nq, LANES).transpose(0, 1, 3, 2, 4).reshape(n, rows * nq, LANES)


def _from_vreg_tiles(x, rows, d):
    n = x.shape[0]
    nq = d // LANES
    return x.reshape(n, rows // 8, nq, 8, LANES).transpose(0, 1, 3, 2, 4).reshape(n, rows, d)


def _conv_layer(x, mod, state, layer, norm_g, w_in, w_dw, b_dw, ln_g, ln_b, w_out, *, ss, il_in, il_out):
    b, t, d = x.shape
    c = w_dw.shape[1]
    k1 = CONV_K - 1
    rows = NSTREAM * ss
    nq = d // LANES
    chained = state is None
    assert ss >= CONV_HALO and ss % 8 == 0
    if chained:
        assert t % rows == 0
        nblk, nt = b, t // rows
        mod3 = mod.reshape(b, 1, 3 * d)
        mod_spec = pl.BlockSpec((1, 1, 3 * d), lambda i, j: (i, 0, 0))
    else:
        assert t == ss and b % NSTREAM == 0
        nblk, nt = b // NSTREAM, 1
        mod3 = mod.reshape(nblk, NSTREAM, 3 * d)
        mod_spec = pl.BlockSpec((1, NSTREAM, 3 * d), lambda i, j: (i, 0, 0))
    xt = x.reshape(nblk * nt, rows, d)
    tile_spec = pl.BlockSpec((1, rows, d), lambda i, j: (i * nt + j, 0, 0))
    vreg_tile_spec = pl.BlockSpec((1, rows * nq, LANES), lambda i, j: (i * nt + j, 0, 0))
    kern = functools.partial(_conv_layer_kernel, ss=ss, d=d, c=c, jb=8, grp=256, tail_chunks=2, chained=chained,
                             il_in=il_in, il_out=il_out)
    inputs = [xt if il_in else _to_vreg_tiles(xt), mod3]
    in_specs = [tile_spec if il_in else vreg_tile_spec, mod_spec]
    if not chained:
        n_layers = state.shape[0]
        st = state.reshape(n_layers, nblk, NSTREAM, k1, c).transpose(0, 1, 3, 2, 4)
        inputs.append(st)
        in_specs.append(pl.BlockSpec((1, 1, k1, NSTREAM, c), lambda i, j: (layer, i, 0, 0, 0)))
    w_rep = jnp.broadcast_to(w_dw[:, None, :], (CONV_K, NSTREAM, c))
    b_rep = jnp.broadcast_to(b_dw[None, :], (NSTREAM, c))
    inputs += [norm_g.reshape(1, d), w_in, w_rep, b_rep, ln_g.reshape(1, c), ln_b.reshape(1, c), w_out]
    in_specs += [
        _const_spec((1, d)),
        _const_spec((d, 3 * c)),
        _const_spec((CONV_K, NSTREAM, c)),
        _const_spec((NSTREAM, c)),
        _const_spec((1, c)),
        _const_spec((1, c)),
        _const_spec((c, d)),
    ]
    scratch = [
        pltpu.VMEM((ss, NSTREAM, d), F32),
        pltpu.VMEM((CONV_HALO + ss, NSTREAM, c), F32),
        pltpu.VMEM((ss, NSTREAM, c), F32),
        pltpu.VMEM((rows, c), BF16),
    ]
    if chained:
        scratch.append(pltpu.VMEM((CONV_HALO, NSTREAM, c), F32))
    xo_shape = (nblk * nt, rows, d) if il_out else (nblk * nt, rows * nq, LANES)
    xo, bufo = pl.pallas_call(
        kern,
        out_shape=(jax.ShapeDtypeStruct(xo_shape, F32), jax.ShapeDtypeStruct((nblk, k1, NSTREAM, c), F32)),
        grid=(nblk, nt),
        in_specs=in_specs,
        out_specs=(
            tile_spec if il_out else vreg_tile_spec,
            pl.BlockSpec((1, k1, NSTREAM, c), lambda i, j: (i, 0, 0, 0)),
        ),
        scratch_shapes=scratch,
        compiler_params=pltpu.CompilerParams(dimension_semantics=("arbitrary", "arbitrary"),
                                             vmem_limit_bytes=VMEM_LIMIT_BYTES),
        name="conv_layer",
    )(*inputs)
    if chained:
        new_buf = bufo[:, :, 0, :]
    else:
        new_buf = bufo.transpose(0, 2, 1, 3).reshape(b, k1, c)
    if not il_out:
        xo = _from_vreg_tiles(xo, rows, d)
    return xo.reshape(b, t, d), new_buf


def _mlstm_layer_kernel(*refs, ll, d, w, dh, has_state, aliased, interleaved):
    it = iter(refs)
    x_ref, mod_ref = next(it), next(it)
    if has_state:
        mbuf_ref, c0_ref, n0_ref, m0_ref = next(it), next(it), next(it), next(it)
    (ng_ref, win_ref, wc_ref, bc_ref, wq_ref, wk_ref, wv_ref, wg_ref, bg_ref, gng_ref, skip_ref,
     wout_ref) = [next(it) for _ in range(12)]
    if aliased:
        next(it)
    xo_ref, mbufo_ref, co_ref, no_ref, mo_ref = [next(it) for _ in range(5)]
    xbuf, xc_s, q_s, k_s, v_s, zs_s, og_s, hh_s = [next(it) for _ in range(8)]
    if interleaved:
        prev_s = next(it)

    t = pl.program_id(1)
    nt = pl.num_programs(1)
    nh = M_HEADS

    @pl.when(t == 0)
    def _():
        if has_state:
            xbuf[MCONV_HALO - (M_CONV_K - 1):MCONV_HALO, :] = mbuf_ref[0, 0]
            co_ref[...] = c0_ref[...]
            no_ref[...] = n0_ref[0]
            mo_ref[...] = m0_ref[0]
        else:
            if interleaved:
                prev_s[...] = jnp.zeros(prev_s.shape, F32)
            else:
                xbuf[0:MCONV_HALO, :] = jnp.zeros((MCONV_HALO, w), F32)
            co_ref[...] = jnp.zeros(co_ref.shape, F32)
            no_ref[...] = jnp.zeros(no_ref.shape, F32)
            mo_ref[...] = jnp.zeros(mo_ref.shape, F32)

    x = x_ref[0]
    mod = mod_ref[0]
    shift = mod[:, 0:d]
    scale = mod[:, d:2 * d]
    gate = mod[:, 2 * d:3 * d]
    hb = _modulated_rms_norm(x, ng_ref[...], shift, scale).astype(BF16)
    xm = jnp.dot(hb, win_ref[:, 0:w], preferred_element_type=F32)
    zs_s[...] = _silu(jnp.dot(hb, win_ref[:, w:2 * w], preferred_element_type=F32)).astype(BF16)
    og_s[...] = _sigmoid(jnp.dot(hb, win_ref[:, 2 * w:3 * w], preferred_element_type=F32)).astype(BF16)

    base = MCONV_HALO - (M_CONV_K - 1)
    if interleaved:
        ssm = ll // NSTREAM
        xbuf[MCONV_HALO:MCONV_HALO + ssm] = xm.reshape(ssm, NSTREAM, w)
        rolled = pltpu.roll(xbuf[ssm:ssm + MCONV_HALO], 1, axis=1)
        sub = lax.broadcasted_iota(jnp.int32, rolled.shape, 1)
        xbuf[0:MCONV_HALO] = jnp.where(sub == 0, prev_s[...], rolled)
        prev_s[...] = rolled
        xc = jnp.broadcast_to(bc_ref[...][None], (ssm, NSTREAM, w))
        for k in range(M_CONV_K):
            xc = xc + wc_ref[k:k + 1, :][None] * xbuf[base + k:base + k + ssm]
        xc = _silu(xc).reshape(ll, w)
    else:
        xbuf[MCONV_HALO:MCONV_HALO + ll, :] = xm
        xc = jnp.broadcast_to(bc_ref[...], (ll, w))
        for k in range(M_CONV_K):
            xc = xc + wc_ref[k:k + 1, :] * xbuf[base + k:base + k + ll, :]
        xc = _silu(xc)
    xc_s[...] = xc

    xc_b = xc.astype(BF16)
    xm_b = xm.astype(BF16)
    for hd in range(nh):
        sl = slice(hd * dh, (hd + 1) * dh)
        q_s[:, sl] = jnp.dot(xc_b[:, sl], wq_ref[hd], preferred_element_type=F32).astype(BF16)
        k_s[:, sl] = jnp.dot(xc_b[:, sl], wk_ref[hd], preferred_element_type=F32).astype(BF16)
        v_s[:, sl] = jnp.dot(xm_b[:, sl], wv_ref[hd], preferred_element_type=F32).astype(BF16)

    gpre = (jnp.dot(q_s[...], wg_ref[0], preferred_element_type=F32)
            + jnp.dot(k_s[...], wg_ref[1], preferred_element_type=F32)
            + jnp.dot(v_s[...], wg_ref[2], preferred_element_type=F32)
            + bg_ref[...])
    lf = _log_sigmoid(gpre)
    row = lax.broadcasted_iota(jnp.int32, (ll, ll), 0)
    col = lax.broadcasted_iota(jnp.int32, (ll, ll), 1)
    if interleaved:
        shift_bits = NSTREAM.bit_length() - 1
        row = jnp.bitwise_and(row, NSTREAM - 1) * ssm + jnp.right_shift(row, shift_bits)
        col = jnp.bitwise_and(col, NSTREAM - 1) * ssm + jnp.right_shift(col, shift_bits)
    causal = col <= row
    tril = jnp.where(causal, 1.0, 0.0).astype(F32)
    bcum = jnp.dot(tril, lf, preferred_element_type=F32, precision=HIGHEST)
    ident = jnp.where(lax.broadcasted_iota(jnp.int32, (GATE_PAD, GATE_PAD), 0)
                      == lax.broadcasted_iota(jnp.int32, (GATE_PAD, GATE_PAD), 1), 1.0, 0.0).astype(F32)
    nt_dims = (((1,), (1,)), ((), ()))
    gpre_t = lax.dot_general(ident, gpre, nt_dims, preferred_element_type=F32, precision=HIGHEST)
    bcum_t = lax.dot_general(ident, bcum, nt_dims, preferred_element_type=F32, precision=HIGHEST)

    kscale = dh ** -0.5
    for hd in range(nh):
        sl = slice(hd * dh, (hd + 1) * dh)
        ig_c = gpre[:, hd:hd + 1]
        b_c = bcum[:, nh + hd:nh + hd + 1]
        ig_r = gpre_t[hd:hd + 1, :]
        b_r = bcum_t[nh + hd:nh + hd + 1, :]
        m_prev = mo_ref[0, :, hd:hd + 1]
        c_prev = co_ref[0, 0, hd]
        n_prev = no_ref[0, hd:hd + 1, :]
        qb = q_s[:, sl]
        kb = k_s[:, sl]
        vb = v_s[:, sl]

        a = b_c + m_prev
        logw = jnp.where(causal, b_c - b_r + ig_r, -jnp.inf)
        mt = jnp.maximum(a, jnp.max(logw, axis=-1, keepdims=True))
        w_inter = jnp.exp(a - mt)
        w_intra = jnp.exp(logw - mt) * kscale
        s = lax.dot_general(qb, kb, nt_dims, preferred_element_type=F32) * w_intra
        inter = lax.dot_general(qb, c_prev.astype(BF16), nt_dims, preferred_element_type=F32)
        num = w_inter * inter + jnp.dot(s.astype(BF16), vb, preferred_element_type=F32)
        qn = jnp.sum(qb.astype(F32) * n_prev, axis=-1, keepdims=True)
        den = w_inter * qn + jnp.sum(s, axis=-1, keepdims=True)
        den = jnp.maximum(jnp.abs(den), jnp.exp(-mt))
        hh_s[:, sl] = num * (1.0 / den)

        m_new = mt[ll - 1:ll, :]
        b_last = b_c[ll - 1:ll, :]
        decay = jnp.exp(b_last + m_prev - m_new)
        wk_c = jnp.exp(b_last - b_c + ig_c - m_new) * kscale
        vw = (vb.astype(F32) * wk_c).astype(BF16)
        upd = lax.dot_general(vw, kb, (((0,), (0,)), ((), ())), preferred_element_type=F32)
        co_ref[0, 0, hd] = decay * c_prev + upd
        no_ref[0, hd:hd + 1, :] = decay * n_prev + jnp.sum(kb.astype(F32) * wk_c, axis=0, keepdims=True)
        mo_ref[0, :, hd:hd + 1] = m_new

    for hd in range(nh):
        sl = slice(hd * dh, (hd + 1) * dh)
        hv = og_s[:, sl].astype(F32) * hh_s[:, sl]
        mu = jnp.mean(hv, axis=-1, keepdims=True)
        hc = hv - mu
        var = jnp.mean(hc * hc, axis=-1, keepdims=True)
        hn = hc * lax.rsqrt(var + EPS) * gng_ref[:, sl]
        hh_s[:, sl] = (hn + skip_ref[:, sl] * xc_s[:, sl]) * zs_s[:, sl].astype(F32)
    out = jnp.dot(hh_s[...].astype(BF16), wout_ref[...], preferred_element_type=F32)
    xo_ref[0] = x + gate * out

    if interleaved:
        @pl.when(t == nt - 1)
        def _():
            mbufo_ref[0] = prev_s[base:MCONV_HALO]
    else:
        @pl.when(t == nt - 1)
        def _():
            mbufo_ref[0] = xbuf[ll + base:ll + MCONV_HALO, :]

        @pl.when(t < nt - 1)
        def _():
            xbuf[0:MCONV_HALO, :] = xbuf[ll:ll + MCONV_HALO, :]


def _mlstm_layer(x, mod, state, c_stack, layer, n_layers, norm_g, w_in, w_conv, b_conv, w_q, w_k, w_v, w_gate_pad,
                 b_gate_pad, gn_g, skip, w_out, *, ll, interleaved):
    b, t, d = x.shape
    w = w_conv.shape[1]
    nh = M_HEADS
    dh = w // nh
    assert t % ll == 0 and ll % 8 == 0 and ll >= MCONV_HALO
    has_state = state is not None
    aliased = c_stack is not None
    km = M_CONV_K - 1
    assert not (interleaved and has_state) and (not interleaved or ll // NSTREAM >= MCONV_HALO)
    kern = functools.partial(_mlstm_layer_kernel, ll=ll, d=d, w=w, dh=dh, has_state=has_state, aliased=aliased,
                             interleaved=interleaved)
    inputs = [x, mod.reshape(b, 1, 3 * d)]
    in_specs = [
        pl.BlockSpec((1, ll, d), lambda i, j: (i, j, 0)),
        pl.BlockSpec((1, 1, 3 * d), lambda i, j: (i, 0, 0)),
    ]
    if has_state:
        mbuf, c0, n0, m0 = state
        inputs += [mbuf, c0, n0, m0.reshape(m0.shape[0], b, 1, nh)]
        in_specs += [
            pl.BlockSpec((1, 1, km, w), lambda i, j: (layer, i, 0, 0)),
            pl.BlockSpec((1, 1, nh, dh, dh), lambda i, j: (layer, i, 0, 0, 0)),
            pl.BlockSpec((1, 1, nh, dh), lambda i, j: (layer, i, 0, 0)),
            pl.BlockSpec((1, 1, 1, nh), lambda i, j: (layer, i, 0, 0)),
        ]
    inputs += [norm_g.reshape(1, d), w_in, w_conv, b_conv.reshape(1, w), w_q, w_k, w_v, w_gate_pad, b_gate_pad,
               gn_g.reshape(1, w), skip.reshape(1, w), w_out]
    in_specs += [
        _const_spec((1, d)),
        _const_spec((d, 3 * w)),
        _const_spec((M_CONV_K, w)),
        _const_spec((1, w)),
        _const_spec((nh, dh, dh)),
        _const_spec((nh, dh, dh)),
        _const_spec((nh, dh, dh)),
        _const_spec((3, w, GATE_PAD)),
        _const_spec((1, GATE_PAD)),
        _const_spec((1, w)),
        _const_spec((1, w)),
        _const_spec((w, d)),
    ]
    aliases = {}
    if aliased:
        aliases = {len(inputs): 2}
        inputs.append(c_stack)
        in_specs.append(pl.BlockSpec(memory_space=pl.ANY))
    if interleaved:
        mbuf_shape, mbuf_spec = (b, km, NSTREAM, w), pl.BlockSpec((1, km, NSTREAM, w), lambda i, j: (i, 0, 0, 0))
        conv_scratch = [pltpu.VMEM((MCONV_HALO + ll // NSTREAM, NSTREAM, w), F32)]
    else:
        mbuf_shape, mbuf_spec = (b, km, w), pl.BlockSpec((1, km, w), lambda i, j: (i, 0, 0))
        conv_scratch = [pltpu.VMEM((MCONV_HALO + ll, w), F32)]
    scratch = conv_scratch + [
        pltpu.VMEM((ll, w), F32),
        pltpu.VMEM((ll, w), BF16),
        pltpu.VMEM((ll, w), BF16),
        pltpu.VMEM((ll, w), BF16),
        pltpu.VMEM((ll, w), BF16),
        pltpu.VMEM((ll, w), BF16),
        pltpu.VMEM((ll, w), F32),
    ]
    if interleaved:
        scratch.append(pltpu.VMEM((MCONV_HALO, NSTREAM, w), F32))
    xo, nb, c_new, n_new, m_new = pl.pallas_call(
        kern,
        out_shape=(
            jax.ShapeDtypeStruct((b, t, d), F32),
            jax.ShapeDtypeStruct(mbuf_shape, F32),
            jax.ShapeDtypeStruct((n_layers, b, nh, dh, dh), F32),
            jax.ShapeDtypeStruct((b, nh, dh), F32),
            jax.ShapeDtypeStruct((b, 1, nh), F32),
        ),
        grid=(b, t // ll),
        in_specs=in_specs,
        out_specs=(
            pl.BlockSpec((1, ll, d), lambda i, j: (i, j, 0)),
            mbuf_spec,
            pl.BlockSpec((1, 1, nh, dh, dh), lambda i, j: (layer, i, 0, 0, 0)),
            pl.BlockSpec((1, nh, dh), lambda i, j: (i, 0, 0)),
            pl.BlockSpec((1, 1, nh), lambda i, j: (i, 0, 0)),
        ),
        scratch_shapes=scratch,
        input_output_aliases=aliases,
        compiler_params=pltpu.CompilerParams(dimension_semantics=("arbitrary", "arbitrary"),
                                             vmem_limit_bytes=VMEM_LIMIT_BYTES),
        name="mlstm_layer",
    )(*inputs)
    if interleaved:
        nb = nb[:, :, 0, :]
    return xo, nb, c_new, n_new, m_new.reshape(b, nh)


def _final_norm_kernel(x_ref, g_ref, o_ref, *scratch, tile_rows, d, interleaved):
    if interleaved:
        (xs,) = scratch
        nq = d // LANES
        ssn = tile_rows // NSTREAM
        for u in range(xs.shape[0] // tile_rows):
            for s in range(NSTREAM):
                for j0 in range(0, ssn, 8):
                    for q in range(nq):
                        f0 = u * tile_rows + s * ssn + j0
                        xs[f0:f0 + 8, q * LANES:(q + 1) * LANES] = x_ref[
                            0, pl.ds(((u * ssn + j0) * nq + q) * NSTREAM + s, 8, stride=nq * NSTREAM), :]
        x = xs[...]
    else:
        x = x_ref[...]
    o_ref[...] = x * lax.rsqrt(jnp.mean(x * x, axis=-1, keepdims=True) + EPS) * g_ref[...]


def _final_norm(x, g, *, tile_rows, rows_blk=1024):
    b, t, d = x.shape
    rows = b * t
    rows_blk = min(rows_blk, rows)
    interleaved = tile_rows is not None
    kern = functools.partial(_final_norm_kernel, tile_rows=tile_rows, d=d, interleaved=interleaved)
    if interleaved:
        nq = d // LANES
        xin = _to_vreg_tiles(x.reshape(rows // rows_blk, rows_blk, d))
        x_spec = pl.BlockSpec((1, rows_blk * nq, LANES), lambda i: (i, 0, 0))
        scratch = [pltpu.VMEM((rows_blk, d), F32)]
    else:
        xin = x.reshape(rows, d)
        x_spec = pl.BlockSpec((rows_blk, d), lambda i: (i, 0))
        scratch = []
    y = pl.pallas_call(
        kern,
        out_shape=jax.ShapeDtypeStruct((rows, d), F32),
        grid=(rows // rows_blk,),
        in_specs=[x_spec, pl.BlockSpec((1, d), lambda i: (0, 0))],
        out_specs=pl.BlockSpec((rows_blk, d), lambda i: (i, 0)),
        scratch_shapes=scratch,
        compiler_params=pltpu.CompilerParams(dimension_semantics=("arbitrary",)),
        name="final_norm",
    )(xin, g.reshape(1, d))
    return y.reshape(b, t, d)


def _prep_weights(P):
    n_ml, _, nh, dh, ng = P['ml_w_gate'].shape
    wg = P['ml_w_gate'].reshape(n_ml, 3, nh * dh, ng)
    wg = jnp.pad(wg, ((0, 0), (0, 0), (0, 0), (0, GATE_PAD - ng))).astype(BF16)
    bg = jnp.pad(P['ml_b_gate'], ((0, 0), (0, GATE_PAD - ng))).reshape(n_ml, 1, GATE_PAD)
    return dict(
        cv_w_in=P['cv_w_in'].astype(BF16), cv_w_out=P['cv_w_out'].astype(BF16),
        ml_w_in=P['ml_w_in'].astype(BF16), ml_w_q=P['ml_w_q'].astype(BF16), ml_w_k=P['ml_w_k'].astype(BF16),
        ml_w_v=P['ml_w_v'].astype(BF16), ml_w_out=P['ml_w_out'].astype(BF16), ml_w_gate=wg, ml_b_gate=bg)


def _run_trunk(x, mods, state, P, W, *, conv_ss, ml_ll):
    depth = P['norm_g'].shape[0]
    n_ml = P['ml_w_in'].shape[0]
    il = state is None
    if il:
        assert conv_ss * NSTREAM == ml_ll
    new_conv, new_mconv, new_n, new_m = [], [], [], []
    c_stack = None
    for i in range(depth):
        j = i // 2
        if i % 2 == 0:
            x, nb = _conv_layer(x, mods[i], None if state is None else state[0], j, P['norm_g'][i], W['cv_w_in'][j],
                                P['cv_w_dw'][j], P['cv_b_dw'][j], P['cv_ln_g'][j], P['cv_ln_b'][j], W['cv_w_out'][j],
                                ss=conv_ss, il_in=il and i > 0, il_out=il)
            new_conv.append(nb)
        else:
            x, nb, c_stack, n, m = _mlstm_layer(
                x, mods[i], None if state is None else state[1:], c_stack, j, n_ml, P['norm_g'][i], W['ml_w_in'][j],
                P['ml_w_conv'][j], P['ml_b_conv'][j], W['ml_w_q'][j], W['ml_w_k'][j], W['ml_w_v'][j],
                W['ml_w_gate'][j], W['ml_b_gate'][j], P['ml_gn_g'][j], P['ml_skip'][j], W['ml_w_out'][j],
                ll=ml_ll, interleaved=il)
            new_mconv.append(nb)
            new_n.append(n)
            new_m.append(m)
    y = _final_norm(x, P['final_g'], tile_rows=ml_ll if il else None)
    return y, jnp.stack(new_conv), jnp.stack(new_mconv), c_stack, jnp.stack(new_n), jnp.stack(new_m)


def kernel(x_prompt, x_sample, c_prompt, c_sample, state_conv, state_mconv, state_C, state_n, state_m, norm_g, ada_w, ada_b, cv_w_in, cv_w_dw, cv_b_dw, cv_ln_g, cv_ln_b, cv_w_out, ml_w_in, ml_w_conv, ml_b_conv, ml_w_q, ml_w_k, ml_w_v, ml_w_gate, ml_b_gate, ml_gn_g, ml_skip, ml_w_out, final_g):
    P = dict(norm_g=norm_g, cv_w_in=cv_w_in, cv_w_dw=cv_w_dw, cv_b_dw=cv_b_dw, cv_ln_g=cv_ln_g, cv_ln_b=cv_ln_b,
             cv_w_out=cv_w_out, ml_w_in=ml_w_in, ml_w_conv=ml_w_conv, ml_b_conv=ml_b_conv, ml_w_q=ml_w_q,
             ml_w_k=ml_w_k, ml_w_v=ml_w_v, ml_w_gate=ml_w_gate, ml_b_gate=ml_b_gate, ml_gn_g=ml_gn_g,
             ml_skip=ml_skip, ml_w_out=ml_w_out, final_g=final_g)
    W = _prep_weights(P)
    bp, tp, _ = x_prompt.shape
    bs, ts, _ = x_sample.shape

    mods = _ada_call(jnp.concatenate([c_prompt, c_sample], axis=0), ada_w, ada_b)
    mods_p = mods[:, :bp]
    mods_s = mods[:, bp:]

    outs_p = _run_trunk(x_prompt, mods_p, None, P, W, conv_ss=PROMPT_TILE // NSTREAM, ml_ll=PROMPT_TILE)
    outs_s = _run_trunk(x_sample, mods_s, (state_conv, state_mconv, state_C, state_n, state_m), P, W,
                        conv_ss=ts, ml_ll=ts)
    return (outs_p[0], outs_s[0]) + tuple(outs_p[1:]) + tuple(outs_s[1:])
```

```python
import functools

import jax
import jax.numpy as jnp
from jax import lax
from jax.experimental import pallas as pl
from jax.experimental.pallas import tpu as pltpu

EPS = 1e-6
CONV_K = 31
M_CONV_K = 4
M_HEADS = 4
GATE_PAD = 128
CONV_HALO = 32
MCONV_HALO = 8
NSTREAM = 8
LANES = 128
PROMPT_TILE = 256
VMEM_LIMIT_BYTES = 58 * 1024 * 1024

F32 = jnp.float32
BF16 = jnp.bfloat16
HIGHEST = lax.Precision.HIGHEST


def _const_spec(shape):
    nd = len(shape)
    return pl.BlockSpec(shape, lambda *_: (0,) * nd, pipeline_mode=pl.Buffered(1))


def _sigmoid(x):
    return 1.0 / (1.0 + jnp.exp(-x))


def _silu(x):
    return x * _sigmoid(x)


def _log_sigmoid(x):
    return jnp.minimum(x, 0.0) - jnp.log1p(jnp.exp(-jnp.abs(x)))


def _modulated_rms_norm(x, g, shift, scale):
    y = x * lax.rsqrt(jnp.mean(x * x, axis=-1, keepdims=True) + EPS)
    return (y * g) * (1.0 + scale) + shift


def _ada_kernel(c_ref, w_ref, b_ref, o_ref):
    o_ref[0] = jnp.dot(c_ref[...], w_ref[0], preferred_element_type=F32, precision=HIGHEST) + b_ref[0]


def _ada_call(c_all, ada_w, ada_b):
    depth, d, e = ada_w.shape
    nb = c_all.shape[0]
    tn = 1024
    return pl.pallas_call(
        _ada_kernel,
        out_shape=jax.ShapeDtypeStruct((depth, nb, e), F32),
        grid=(depth, e // tn),
        in_specs=[
            pl.BlockSpec((nb, d), lambda i, j: (0, 0)),
            pl.BlockSpec((1, d, tn), lambda i, j: (i, 0, j)),
            pl.BlockSpec((1, 1, tn), lambda i, j: (i, 0, j)),
        ],
        out_specs=pl.BlockSpec((1, nb, tn), lambda i, j: (i, 0, j)),
        compiler_params=pltpu.CompilerParams(dimension_semantics=("arbitrary", "arbitrary")),
        name="ada_mod",
    )(c_all, ada_w, ada_b.reshape(depth, 1, e))


def _conv_layer_kernel(*refs, ss, d, c, jb, grp, tail_chunks, chained, il_in, il_out):
    it = iter(refs)
    x_ref, mod_ref = next(it), next(it)
    if not chained:
        buf_ref = next(it)
    ng_ref, win_ref, wdw_ref, bdw_ref, lng_ref, lnb_ref, wout_ref = [next(it) for _ in range(7)]
    xo_ref, bufo_ref = next(it), next(it)
    xp_s, gbuf, ybuf, zs_s = next(it), next(it), next(it), next(it)
    if chained:
        prev_s = next(it)

    t = pl.program_id(1)
    nt = pl.num_programs(1)
    rows = ss * NSTREAM
    halo0 = CONV_HALO - (CONV_K - 1)

    nq = d // LANES
    if il_in:
        xp_s[...] = x_ref[0].reshape(ss, NSTREAM, d)
    else:
        for j in range(ss):
            for q in range(nq):
                xp_s[j, :, q * LANES:(q + 1) * LANES] = x_ref[
                    0, pl.ds((j // 8 * nq + q) * 8 + j % 8, NSTREAM, stride=ss * nq), :]
    x3 = xp_s[...]
    mod = mod_ref[...]
    shift = mod[:, :, 0:d]
    scale = mod[:, :, d:2 * d]
    gate = mod[:, :, 2 * d:3 * d]
    hb = _modulated_rms_norm(x3, ng_ref[...], shift, scale).reshape(rows, d).astype(BF16)

    if chained:
        @pl.when(t == 0)
        def _():
            prev_s[...] = jnp.zeros(prev_s.shape, F32)

    for g0 in range(0, c, grp):
        gs = slice(g0, g0 + grp)
        a = jnp.dot(hb, win_ref[:, g0:g0 + grp], preferred_element_type=F32)
        ag = jnp.dot(hb, win_ref[:, c + g0:c + g0 + grp], preferred_element_type=F32)
        gbuf[CONV_HALO:CONV_HALO + ss, :, gs] = (a * _sigmoid(ag)).reshape(ss, NSTREAM, grp)

        if chained:
            rolled = pltpu.roll(gbuf[ss:ss + CONV_HALO, :, gs], 1, axis=1)
            sub = lax.broadcasted_iota(jnp.int32, rolled.shape, 1)
            gbuf[0:CONV_HALO, :, gs] = jnp.where(sub == 0, prev_s[:, :, gs], rolled)
            prev_s[:, :, gs] = rolled
        else:
            gbuf[halo0:CONV_HALO, :, gs] = buf_ref[0, 0, :, :, gs]

        for c0 in range(g0, g0 + grp, LANES):
            cs = slice(c0, c0 + LANES)
            wb = [wdw_ref[k, :, cs] for k in range(CONV_K)]
            for j0 in range(0, ss, jb):
                acc = [bdw_ref[:, cs] for _ in range(jb)]
                for i in range(jb + CONV_K - 1):
                    xin = gbuf[halo0 + j0 + i, :, cs]
                    for jj in range(jb):
                        k = i - jj
                        if 0 <= k < CONV_K:
                            acc[jj] = acc[jj] + wb[k] * xin
                for jj in range(jb):
                    ybuf[j0 + jj, :, cs] = acc[jj]

    zs_s[...] = _silu(jnp.dot(hb, win_ref[:, 2 * c:3 * c], preferred_element_type=F32)).astype(BF16)

    jc = ss // tail_chunks
    for j0 in range(0, ss, jc):
        y = ybuf[j0:j0 + jc].reshape(jc * NSTREAM, c)
        mu = jnp.mean(y, axis=-1, keepdims=True)
        yc = y - mu
        var = jnp.mean(yc * yc, axis=-1, keepdims=True)
        y = yc * lax.rsqrt(var + EPS) * lng_ref[...] + lnb_ref[...]
        y = _silu(y) * zs_s[j0 * NSTREAM:(j0 + jc) * NSTREAM, :].astype(F32)
        out = jnp.dot(y.astype(BF16), wout_ref[...], preferred_element_type=F32)
        xnew = xp_s[j0:j0 + jc] + gate * out.reshape(jc, NSTREAM, d)
        if il_out:
            xo_ref[0, j0 * NSTREAM:(j0 + jc) * NSTREAM, :] = xnew.reshape(jc * NSTREAM, d)
        else:
            xp_s[j0:j0 + jc] = xnew
            for j in range(j0, j0 + jc):
                for q in range(nq):
                    xo_ref[0, pl.ds((j // 8 * nq + q) * 8 + j % 8, NSTREAM, stride=ss * nq), :] = xp_s[
                        j, :, q * LANES:(q + 1) * LANES]

    if chained:
        @pl.when(t == nt - 1)
        def _():
            bufo_ref[0] = prev_s[halo0:CONV_HALO]
    else:
        bufo_ref[0] = gbuf[ss + halo0:ss + CONV_HALO]


def _to_vreg_tiles(x):
    n, rows, d = x.shape
    nq = d // LANES
    return x.reshape(n, rows // 8, 8, nq, LANES).transpose(0, 1, 3, 2, 4).reshape(n, rows * nq, LANES)


def _from_vreg_tiles(x, rows, d):
    n = x.shape[0]
    nq = d // LANES
    return x.reshape(n, rows // 8, nq, 8, LANES).transpose(0, 1, 3, 2, 4).reshape(n, rows, d)


def _conv_layer(x, mod, state, layer, norm_g, w_in, w_dw, b_dw, ln_g, ln_b, w_out, *, ss, il_in, il_out):
    b, t, d = x.shape
    c = w_dw.shape[1]
    k1 = CONV_K - 1
    rows = NSTREAM * ss
    nq = d // LANES
    chained = state is None
    assert ss >= CONV_HALO and ss % 8 == 0
    if chained:
        assert t % rows == 0
        nblk, nt = b, t // rows
        mod3 = mod.reshape(b, 1, 3 * d)
        mod_spec = pl.BlockSpec((1, 1, 3 * d), lambda i, j: (i, 0, 0))
    else:
        assert t == ss and b % NSTREAM == 0
        nblk, nt = b // NSTREAM, 1
        mod3 = mod.reshape(nblk, NSTREAM, 3 * d)
        mod_spec = pl.BlockSpec((1, NSTREAM, 3 * d), lambda i, j: (i, 0, 0))
    xt = x.reshape(nblk * nt, rows, d)
    tile_spec = pl.BlockSpec((1, rows, d), lambda i, j: (i * nt + j, 0, 0))
    vreg_tile_spec = pl.BlockSpec((1, rows * nq, LANES), lambda i, j: (i * nt + j, 0, 0))
    kern = functools.partial(_conv_layer_kernel, ss=ss, d=d, c=c, jb=8, grp=256, tail_chunks=2, chained=chained,
                             il_in=il_in, il_out=il_out)
    inputs = [xt if il_in else _to_vreg_tiles(xt), mod3]
    in_specs = [tile_spec if il_in else vreg_tile_spec, mod_spec]
    if not chained:
        n_layers = state.shape[0]
        st = state.reshape(n_layers, nblk, NSTREAM, k1, c).transpose(0, 1, 3, 2, 4)
        inputs.append(st)
        in_specs.append(pl.BlockSpec((1, 1, k1, NSTREAM, c), lambda i, j: (layer, i, 0, 0, 0)))
    w_rep = jnp.broadcast_to(w_dw[:, None, :], (CONV_K, NSTREAM, c))
    b_rep = jnp.broadcast_to(b_dw[None, :], (NSTREAM, c))
    inputs += [norm_g.reshape(1, d), w_in, w_rep, b_rep, ln_g.reshape(1, c), ln_b.reshape(1, c), w_out]
    in_specs += [
        _const_spec((1, d)),
        _const_spec((d, 3 * c)),
        _const_spec((CONV_K, NSTREAM, c)),
        _const_spec((NSTREAM, c)),
        _const_spec((1, c)),
        _const_spec((1, c)),
        _const_spec((c, d)),
    ]
    scratch = [
        pltpu.VMEM((ss, NSTREAM, d), F32),
        pltpu.VMEM((CONV_HALO + ss, NSTREAM, c), F32),
        pltpu.VMEM((ss, NSTREAM, c), F32),
        pltpu.VMEM((rows, c), BF16),
    ]
    if chained:
        scratch.append(pltpu.VMEM((CONV_HALO, NSTREAM, c), F32))
    xo_shape = (nblk * nt, rows, d) if il_out else (nblk * nt, rows * nq, LANES)
    xo, bufo = pl.pallas_call(
        kern,
        out_shape=(jax.ShapeDtypeStruct(xo_shape, F32), jax.ShapeDtypeStruct((nblk, k1, NSTREAM, c), F32)),
        grid=(nblk, nt),
        in_specs=in_specs,
        out_specs=(
            tile_spec if il_out else vreg_tile_spec,
            pl.BlockSpec((1, k1, NSTREAM, c), lambda i, j: (i, 0, 0, 0)),
        ),
        scratch_shapes=scratch,
        compiler_params=pltpu.CompilerParams(dimension_semantics=("arbitrary", "arbitrary"),
                                             vmem_limit_bytes=VMEM_LIMIT_BYTES),
        name="conv_layer",
    )(*inputs)
    if chained:
        new_buf = bufo[:, :, 0, :]
    else:
        new_buf = bufo.transpose(0, 2, 1, 3).reshape(b, k1, c)
    if not il_out:
        xo = _from_vreg_tiles(xo, rows, d)
    return xo.reshape(b, t, d), new_buf


def _mlstm_layer_kernel(*refs, ll, d, w, dh, has_state, aliased, interleaved, gate_split):
    it = iter(refs)
    x_ref, mod_ref = next(it), next(it)
    if has_state:
        mbuf_ref, c0_ref, n0_ref, m0_ref = next(it), next(it), next(it), next(it)
    (ng_ref, win_ref, wc_ref, bc_ref, wq_ref, wk_ref, wv_ref, wg_ref, bg_ref, gng_ref, skip_ref,
     wout_ref) = [next(it) for _ in range(12)]
    if aliased:
        next(it)
    xo_ref, mbufo_ref, co_ref, no_ref, mo_ref = [next(it) for _ in range(5)]
    xbuf, xc_s, q_s, k_s, v_s, zs_s, og_s, hh_s = [next(it) for _ in range(8)]
    if interleaved:
        prev_s = next(it)

    t = pl.program_id(1)
    nt = pl.num_programs(1)
    nh = M_HEADS

    @pl.when(t == 0)
    def _():
        if has_state:
            xbuf[MCONV_HALO - (M_CONV_K - 1):MCONV_HALO, :] = mbuf_ref[0, 0]
            co_ref[...] = c0_ref[...]
            no_ref[...] = n0_ref[0]
            mo_ref[...] = m0_ref[0]
        else:
            if interleaved:
                prev_s[...] = jnp.zeros(prev_s.shape, F32)
            else:
                xbuf[0:MCONV_HALO, :] = jnp.zeros((MCONV_HALO, w), F32)
            co_ref[...] = jnp.zeros(co_ref.shape, F32)
            no_ref[...] = jnp.zeros(no_ref.shape, F32)
            mo_ref[...] = jnp.zeros(mo_ref.shape, F32)

    x = x_ref[0]
    mod = mod_ref[0]
    shift = mod[:, 0:d]
    scale = mod[:, d:2 * d]
    gate = mod[:, 2 * d:3 * d]
    hb = _modulated_rms_norm(x, ng_ref[...], shift, scale).astype(BF16)
    xm = jnp.dot(hb, win_ref[:, 0:w], preferred_element_type=F32)
    zs_s[...] = _silu(jnp.dot(hb, win_ref[:, w:2 * w], preferred_element_type=F32)).astype(BF16)
    og_s[...] = _sigmoid(jnp.dot(hb, win_ref[:, 2 * w:3 * w], preferred_element_type=F32)).astype(BF16)

    base = MCONV_HALO - (M_CONV_K - 1)
    if interleaved:
        ssm = ll // NSTREAM
        xbuf[MCONV_HALO:MCONV_HALO + ssm] = xm.reshape(ssm, NSTREAM, w)
        rolled = pltpu.roll(xbuf[ssm:ssm + MCONV_HALO], 1, axis=1)
        sub = lax.broadcasted_iota(jnp.int32, rolled.shape, 1)
        xbuf[0:MCONV_HALO] = jnp.where(sub == 0, prev_s[...], rolled)
        prev_s[...] = rolled
        xc = jnp.broadcast_to(bc_ref[...][None], (ssm, NSTREAM, w))
        for k in range(M_CONV_K):
            xc = xc + wc_ref[k:k + 1, :][None] * xbuf[base + k:base + k + ssm]
        xc = _silu(xc).reshape(ll, w)
    else:
        xbuf[MCONV_HALO:MCONV_HALO + ll, :] = xm
        xc = jnp.broadcast_to(bc_ref[...], (ll, w))
        for k in range(M_CONV_K):
            xc = xc + wc_ref[k:k + 1, :] * xbuf[base + k:base + k + ll, :]
        xc = _silu(xc)
    xc_s[...] = xc

    xc_b = xc.astype(BF16)
    xm_b = xm.astype(BF16)
    for hd in range(nh):
        sl = slice(hd * dh, (hd + 1) * dh)
        q_s[:, sl] = jnp.dot(xc_b[:, sl], wq_ref[hd], preferred_element_type=F32).astype(BF16)
        k_s[:, sl] = jnp.dot(xc_b[:, sl], wk_ref[hd], preferred_element_type=F32).astype(BF16)
        v_s[:, sl] = jnp.dot(xm_b[:, sl], wv_ref[hd], preferred_element_type=F32).astype(BF16)

    gpre = bg_ref[...]
    for op, src_s in enumerate((q_s, k_s, v_s)):
        if gate_split:
            hw = w // 2
            lo = jnp.dot(src_s[:, 0:hw], wg_ref[op], preferred_element_type=F32)
            hi = jnp.dot(src_s[:, hw:w], wg_ref[op], preferred_element_type=F32)
            gpre = gpre + lo[:, 0:GATE_PAD] + hi[:, GATE_PAD:2 * GATE_PAD]
        else:
            gpre = gpre + jnp.dot(src_s[...], wg_ref[op], preferred_element_type=F32)
    lf = _log_sigmoid(gpre)
    row = lax.broadcasted_iota(jnp.int32, (ll, ll), 0)
    col = lax.broadcasted_iota(jnp.int32, (ll, ll), 1)
    if interleaved:
        shift_bits = NSTREAM.bit_length() - 1
        row = jnp.bitwise_and(row, NSTREAM - 1) * ssm + jnp.right_shift(row, shift_bits)
        col = jnp.bitwise_and(col, NSTREAM - 1) * ssm + jnp.right_shift(col, shift_bits)
    causal = col <= row
    nt_dims = (((1,), (1,)), ((), ()))
    if gate_split:
        tril = jnp.where(causal, 1.0, 0.0).astype(BF16)
        lf_hi = lf.astype(BF16)
        r1 = lf - lf_hi.astype(F32)
        lf_mid = r1.astype(BF16)
        lf_lo = (r1 - lf_mid.astype(F32)).astype(BF16)
        bcum = (jnp.dot(tril, lf_hi, preferred_element_type=F32) + jnp.dot(tril, lf_mid, preferred_element_type=F32)
                + jnp.dot(tril, lf_lo, preferred_element_type=F32))
        gpre_t = gpre.T
        bcum_t = bcum.T
    else:
        tril = jnp.where(causal, 1.0, 0.0).astype(F32)
        bcum = jnp.dot(tril, lf, preferred_element_type=F32, precision=HIGHEST)
        ident = jnp.where(lax.broadcasted_iota(jnp.int32, (GATE_PAD, GATE_PAD), 0)
                          == lax.broadcasted_iota(jnp.int32, (GATE_PAD, GATE_PAD), 1), 1.0, 0.0).astype(F32)
        gpre_t = lax.dot_general(ident, gpre, nt_dims, preferred_element_type=F32, precision=HIGHEST)
        bcum_t = lax.dot_general(ident, bcum, nt_dims, preferred_element_type=F32, precision=HIGHEST)

    kscale = dh ** -0.5
    for hd in range(nh):
        sl = slice(hd * dh, (hd + 1) * dh)
        ig_c = gpre[:, hd:hd + 1]
        b_c = bcum[:, nh + hd:nh + hd + 1]
        ig_r = gpre_t[hd:hd + 1, :]
        b_r = bcum_t[nh + hd:nh + hd + 1, :]
        m_prev = mo_ref[0, :, hd:hd + 1]
        c_prev = co_ref[0, 0, hd]
        n_prev = no_ref[0, hd:hd + 1, :]
        qb = q_s[:, sl]
        kb = k_s[:, sl]
        vb = v_s[:, sl]

        a = b_c + m_prev
        logw = jnp.where(causal, b_c - b_r + ig_r, -jnp.inf)
        mt = jnp.maximum(a, jnp.max(logw, axis=-1, keepdims=True))
        w_inter = jnp.exp(a - mt)
        w_intra = jnp.exp(logw - mt) * kscale
        s = lax.dot_general(qb, kb, nt_dims, preferred_element_type=F32) * w_intra
        inter = lax.dot_general(qb, c_prev.astype(BF16), nt_dims, preferred_element_type=F32)
        num = w_inter * inter + jnp.dot(s.astype(BF16), vb, preferred_element_type=F32)
        qn = jnp.sum(qb.astype(F32) * n_prev, axis=-1, keepdims=True)
        den = w_inter * qn + jnp.sum(s, axis=-1, keepdims=True)
        den = jnp.maximum(jnp.abs(den), jnp.exp(-mt))
        hh_s[:, sl] = num * (1.0 / den)

        m_new = mt[ll - 1:ll, :]
        b_last = b_c[ll - 1:ll, :]
        decay = jnp.exp(b_last + m_prev - m_new)
        wk_c = jnp.exp(b_last - b_c + ig_c - m_new) * kscale
        vw = (vb.astype(F32) * wk_c).astype(BF16)
        upd = lax.dot_general(vw, kb, (((0,), (0,)), ((), ())), preferred_element_type=F32)
        co_ref[0, 0, hd] = decay * c_prev + upd
        no_ref[0, hd:hd + 1, :] = decay * n_prev + jnp.sum(kb.astype(F32) * wk_c, axis=0, keepdims=True)
        mo_ref[0, :, hd:hd + 1] = m_new

    for hd in range(nh):
        sl = slice(hd * dh, (hd + 1) * dh)
        hv = og_s[:, sl].astype(F32) * hh_s[:, sl]
        mu = jnp.mean(hv, axis=-1, keepdims=True)
        hc = hv - mu
        var = jnp.mean(hc * hc, axis=-1, keepdims=True)
        hn = hc * lax.rsqrt(var + EPS) * gng_ref[:, sl]
        hh_s[:, sl] = (hn + skip_ref[:, sl] * xc_s[:, sl]) * zs_s[:, sl].astype(F32)
    out = jnp.dot(hh_s[...].astype(BF16), wout_ref[...], preferred_element_type=F32)
    xo_ref[0] = x + gate * out

    if interleaved:
        @pl.when(t == nt - 1)
        def _():
            mbufo_ref[0] = prev_s[base:MCONV_HALO]
    else:
        @pl.when(t == nt - 1)
        def _():
            mbufo_ref[0] = xbuf[ll + base:ll + MCONV_HALO, :]

        @pl.when(t < nt - 1)
        def _():
            xbuf[0:MCONV_HALO, :] = xbuf[ll:ll + MCONV_HALO, :]


def _mlstm_layer(x, mod, state, c_stack, layer, n_layers, norm_g, w_in, w_conv, b_conv, w_q, w_k, w_v, w_gate_pad,
                 b_gate_pad, gn_g, skip, w_out, *, ll, interleaved):
    b, t, d = x.shape
    w = w_conv.shape[1]
    nh = M_HEADS
    dh = w // nh
    assert t % ll == 0 and ll % 8 == 0 and ll >= MCONV_HALO
    has_state = state is not None
    aliased = c_stack is not None
    km = M_CONV_K - 1
    assert not (interleaved and has_state) and (not interleaved or ll // NSTREAM >= MCONV_HALO)
    gate_split = ll % LANES == 0
    if gate_split:
        w_gate_pad = jnp.concatenate([w_gate_pad[:, :w // 2], w_gate_pad[:, w // 2:]], axis=-1)
    kern = functools.partial(_mlstm_layer_kernel, ll=ll, d=d, w=w, dh=dh, has_state=has_state, aliased=aliased,
                             interleaved=interleaved, gate_split=gate_split)
    inputs = [x, mod.reshape(b, 1, 3 * d)]
    in_specs = [
        pl.BlockSpec((1, ll, d), lambda i, j: (i, j, 0)),
        pl.BlockSpec((1, 1, 3 * d), lambda i, j: (i, 0, 0)),
    ]
    if has_state:
        mbuf, c0, n0, m0 = state
        inputs += [mbuf, c0, n0, m0.reshape(m0.shape[0], b, 1, nh)]
        in_specs += [
            pl.BlockSpec((1, 1, km, w), lambda i, j: (layer, i, 0, 0)),
            pl.BlockSpec((1, 1, nh, dh, dh), lambda i, j: (layer, i, 0, 0, 0)),
            pl.BlockSpec((1, 1, nh, dh), lambda i, j: (layer, i, 0, 0)),
            pl.BlockSpec((1, 1, 1, nh), lambda i, j: (layer, i, 0, 0)),
        ]
    inputs += [norm_g.reshape(1, d), w_in, w_conv, b_conv.reshape(1, w), w_q, w_k, w_v, w_gate_pad, b_gate_pad,
               gn_g.reshape(1, w), skip.reshape(1, w), w_out]
    in_specs += [
        _const_spec((1, d)),
        _const_spec((d, 3 * w)),
        _const_spec((M_CONV_K, w)),
        _const_spec((1, w)),
        _const_spec((nh, dh, dh)),
        _const_spec((nh, dh, dh)),
        _const_spec((nh, dh, dh)),
        _const_spec(tuple(w_gate_pad.shape)),
        _const_spec((1, GATE_PAD)),
        _const_spec((1, w)),
        _const_spec((1, w)),
        _const_spec((w, d)),
    ]
    aliases = {}
    if aliased:
        aliases = {len(inputs): 2}
        inputs.append(c_stack)
        in_specs.append(pl.BlockSpec(memory_space=pl.ANY))
    if interleaved:
        mbuf_shape, mbuf_spec = (b, km, NSTREAM, w), pl.BlockSpec((1, km, NSTREAM, w), lambda i, j: (i, 0, 0, 0))
        conv_scratch = [pltpu.VMEM((MCONV_HALO + ll // NSTREAM, NSTREAM, w), F32)]
    else:
        mbuf_shape, mbuf_spec = (b, km, w), pl.BlockSpec((1, km, w), lambda i, j: (i, 0, 0))
        conv_scratch = [pltpu.VMEM((MCONV_HALO + ll, w), F32)]
    scratch = conv_scratch + [
        pltpu.VMEM((ll, w), F32),
        pltpu.VMEM((ll, w), BF16),
        pltpu.VMEM((ll, w), BF16),
        pltpu.VMEM((ll, w), BF16),
        pltpu.VMEM((ll, w), BF16),
        pltpu.VMEM((ll, w), BF16),
        pltpu.VMEM((ll, w), F32),
    ]
    if interleaved:
        scratch.append(pltpu.VMEM((MCONV_HALO, NSTREAM, w), F32))
    xo, nb, c_new, n_new, m_new = pl.pallas_call(
        kern,
        out_shape=(
            jax.ShapeDtypeStruct((b, t, d), F32),
            jax.ShapeDtypeStruct(mbuf_shape, F32),
            jax.ShapeDtypeStruct((n_layers, b, nh, dh, dh), F32),
            jax.ShapeDtypeStruct((b, nh, dh), F32),
            jax.ShapeDtypeStruct((b, 1, nh), F32),
        ),
        grid=(b, t // ll),
        in_specs=in_specs,
        out_specs=(
            pl.BlockSpec((1, ll, d), lambda i, j: (i, j, 0)),
            mbuf_spec,
            pl.BlockSpec((1, 1, nh, dh, dh), lambda i, j: (layer, i, 0, 0, 0)),
            pl.BlockSpec((1, nh, dh), lambda i, j: (i, 0, 0)),
            pl.BlockSpec((1, 1, nh), lambda i, j: (i, 0, 0)),
        ),
        scratch_shapes=scratch,
        input_output_aliases=aliases,
        compiler_params=pltpu.CompilerParams(dimension_semantics=("arbitrary", "arbitrary"),
                                             vmem_limit_bytes=VMEM_LIMIT_BYTES),
        name="mlstm_layer",
    )(*inputs)
    if interleaved:
        nb = nb[:, :, 0, :]
    return xo, nb, c_new, n_new, m_new.reshape(b, nh)


def _final_norm_kernel(x_ref, g_ref, o_ref, *scratch, tile_rows, d, interleaved):
    if interleaved:
        (xs,) = scratch
        nq = d // LANES
        ssn = tile_rows // NSTREAM
        for u in range(xs.shape[0] // tile_rows):
            for s in range(NSTREAM):
                for j0 in range(0, ssn, 8):
                    for q in range(nq):
                        f0 = u * tile_rows + s * ssn + j0
                        xs[f0:f0 + 8, q * LANES:(q + 1) * LANES] = x_ref[
                            0, pl.ds(((u * ssn + j0) * nq + q) * NSTREAM + s, 8, stride=nq * NSTREAM), :]
        x = xs[...]
    else:
        x = x_ref[...]
    o_ref[...] = x * lax.rsqrt(jnp.mean(x * x, axis=-1, keepdims=True) + EPS) * g_ref[...]


def _final_norm(x, g, *, tile_rows, rows_blk=1024):
    b, t, d = x.shape
    rows = b * t
    rows_blk = min(rows_blk, rows)
    interleaved = tile_rows is not None
    kern = functools.partial(_final_norm_kernel, tile_rows=tile_rows, d=d, interleaved=interleaved)
    if interleaved:
        nq = d // LANES
        xin = _to_vreg_tiles(x.reshape(rows // rows_blk, rows_blk, d))
        x_spec = pl.BlockSpec((1, rows_blk * nq, LANES), lambda i: (i, 0, 0))
        scratch = [pltpu.VMEM((rows_blk, d), F32)]
    else:
        xin = x.reshape(rows, d)
        x_spec = pl.BlockSpec((rows_blk, d), lambda i: (i, 0))
        scratch = []
    y = pl.pallas_call(
        kern,
        out_shape=jax.ShapeDtypeStruct((rows, d), F32),
        grid=(rows // rows_blk,),
        in_specs=[x_spec, pl.BlockSpec((1, d), lambda i: (0, 0))],
        out_specs=pl.BlockSpec((rows_blk, d), lambda i: (i, 0)),
        scratch_shapes=scratch,
        compiler_params=pltpu.CompilerParams(dimension_semantics=("arbitrary",)),
        name="final_norm",
    )(xin, g.reshape(1, d))
    return y.reshape(b, t, d)


def _prep_weights(P):
    n_ml, _, nh, dh, ng = P['ml_w_gate'].shape
    wg = P['ml_w_gate'].reshape(n_ml, 3, nh * dh, ng)
    wg = jnp.pad(wg, ((0, 0), (0, 0), (0, 0), (0, GATE_PAD - ng))).astype(BF16)
    bg = jnp.pad(P['ml_b_gate'], ((0, 0), (0, GATE_PAD - ng))).reshape(n_ml, 1, GATE_PAD)
    return dict(
        cv_w_in=P['cv_w_in'].astype(BF16), cv_w_out=P['cv_w_out'].astype(BF16),
        ml_w_in=P['ml_w_in'].astype(BF16), ml_w_q=P['ml_w_q'].astype(BF16), ml_w_k=P['ml_w_k'].astype(BF16),
        ml_w_v=P['ml_w_v'].astype(BF16), ml_w_out=P['ml_w_out'].astype(BF16), ml_w_gate=wg, ml_b_gate=bg)


def _run_trunk(x, mods, state, P, W, *, conv_ss, ml_ll):
    depth = P['norm_g'].shape[0]
    n_ml = P['ml_w_in'].shape[0]
    il = state is None
    if il:
        assert conv_ss * NSTREAM == ml_ll
    new_conv, new_mconv, new_n, new_m = [], [], [], []
    c_stack = None
    for i in range(depth):
        j = i // 2
        if i % 2 == 0:
            x, nb = _conv_layer(x, mods[i], None if state is None else state[0], j, P['norm_g'][i], W['cv_w_in'][j],
                                P['cv_w_dw'][j], P['cv_b_dw'][j], P['cv_ln_g'][j], P['cv_ln_b'][j], W['cv_w_out'][j],
                                ss=conv_ss, il_in=il and i > 0, il_out=il)
            new_conv.append(nb)
        else:
            x, nb, c_stack, n, m = _mlstm_layer(
                x, mods[i], None if state is None else state[1:], c_stack, j, n_ml, P['norm_g'][i], W['ml_w_in'][j],
                P['ml_w_conv'][j], P['ml_b_conv'][j], W['ml_w_q'][j], W['ml_w_k'][j], W['ml_w_v'][j],
                W['ml_w_gate'][j], W['ml_b_gate'][j], P['ml_gn_g'][j], P['ml_skip'][j], W['ml_w_out'][j],
                ll=ml_ll, interleaved=il)
            new_mconv.append(nb)
            new_n.append(n)
            new_m.append(m)
    y = _final_norm(x, P['final_g'], tile_rows=ml_ll if il else None)
    return y, jnp.stack(new_conv), jnp.stack(new_mconv), c_stack, jnp.stack(new_n), jnp.stack(new_m)


def kernel(x_prompt, x_sample, c_prompt, c_sample, state_conv, state_mconv, state_C, state_n, state_m, norm_g, ada_w, ada_b, cv_w_in, cv_w_dw, cv_b_dw, cv_ln_g, cv_ln_b, cv_w_out, ml_w_in, ml_w_conv, ml_b_conv, ml_w_q, ml_w_k, ml_w_v, ml_w_gate, ml_b_gate, ml_gn_g, ml_skip, ml_w_out, final_g):
    P = dict(norm_g=norm_g, cv_w_in=cv_w_in, cv_w_dw=cv_w_dw, cv_b_dw=cv_b_dw, cv_ln_g=cv_ln_g, cv_ln_b=cv_ln_b,
             cv_w_out=cv_w_out, ml_w_in=ml_w_in, ml_w_conv=ml_w_conv, ml_b_conv=ml_b_conv, ml_w_q=ml_w_q,
             ml_w_k=ml_w_k, ml_w_v=ml_w_v, ml_w_gate=ml_w_gate, ml_b_gate=ml_b_gate, ml_gn_g=ml_gn_g,
             ml_skip=ml_skip, ml_w_out=ml_w_out, final_g=final_g)
    W = _prep_weights(P)
    bp, tp, _ = x_prompt.shape
    bs, ts, _ = x_sample.shape

    mods = _ada_call(jnp.concatenate([c_prompt, c_sample], axis=0), ada_w, ada_b)
    mods_p = mods[:, :bp]
    mods_s = mods[:, bp:]

    outs_p = _run_trunk(x_prompt, mods_p, None, P, W, conv_ss=PROMPT_TILE // NSTREAM, ml_ll=PROMPT_TILE)
    outs_s = _run_trunk(x_sample, mods_s, (state_conv, state_mconv, state_C, state_n, state_m), P, W,
                        conv_ss=ts, ml_ll=ts)
    return (outs_p[0], outs_s[0]) + tuple(outs_p[1:]) + tuple(outs_s[1:])
```

```python
import functools

import jax
import jax.numpy as jnp
from jax import lax
from jax.experimental import pallas as pl
from jax.experimental.pallas import tpu as pltpu

EPS = 1e-6
CONV_K = 31
M_CONV_K = 4
M_HEADS = 4
GATE_PAD = 128
CONV_HALO = 32
MCONV_HALO = 8
NSTREAM = 8
LANES = 128
CONV_NSEQ = 2
PROMPT_TILE = 256
VMEM_LIMIT_BYTES = 58 * 1024 * 1024

F32 = jnp.float32
BF16 = jnp.bfloat16
HIGHEST = lax.Precision.HIGHEST


def _const_spec(shape):
    nd = len(shape)
    return pl.BlockSpec(shape, lambda *_: (0,) * nd, pipeline_mode=pl.Buffered(1))


def _sigmoid(x):
    return 1.0 / (1.0 + jnp.exp(-x))


def _silu(x):
    return x * _sigmoid(x)


def _log_sigmoid(x):
    return jnp.minimum(x, 0.0) - jnp.log1p(jnp.exp(-jnp.abs(x)))


def _modulated_rms_norm(x, g, shift, scale):
    y = x * lax.rsqrt(jnp.mean(x * x, axis=-1, keepdims=True) + EPS)
    return (y * g) * (1.0 + scale) + shift


def _ada_kernel(c_ref, w_ref, b_ref, o_ref):
    o_ref[0] = jnp.dot(c_ref[...], w_ref[0], preferred_element_type=F32, precision=HIGHEST) + b_ref[0]


def _ada_call(c_all, ada_w, ada_b):
    depth, d, e = ada_w.shape
    nb = c_all.shape[0]
    tn = 1024
    return pl.pallas_call(
        _ada_kernel,
        out_shape=jax.ShapeDtypeStruct((depth, nb, e), F32),
        grid=(depth, e // tn),
        in_specs=[
            pl.BlockSpec((nb, d), lambda i, j: (0, 0)),
            pl.BlockSpec((1, d, tn), lambda i, j: (i, 0, j)),
            pl.BlockSpec((1, 1, tn), lambda i, j: (i, 0, j)),
        ],
        out_specs=pl.BlockSpec((1, nb, tn), lambda i, j: (i, 0, j)),
        compiler_params=pltpu.CompilerParams(dimension_semantics=("arbitrary", "arbitrary")),
        name="ada_mod",
    )(c_all, ada_w, ada_b.reshape(depth, 1, e))


def _conv_layer_kernel(*refs, ss, d, c, jb, grp, tail_chunks, chained, il_in, il_out, nseq):
    it = iter(refs)
    x_ref, mod_ref = next(it), next(it)
    if not chained:
        buf_ref = next(it)
    ng_ref, win_ref, wdw_ref, bdw_ref, lng_ref, lnb_ref, wout_ref = [next(it) for _ in range(7)]
    xo_ref, bufo_ref = next(it), next(it)
    xp_l = [next(it) for _ in range(nseq)]
    gbuf_l = [next(it) for _ in range(nseq)]
    ybuf_l = [next(it) for _ in range(nseq)]
    zs_l = [next(it) for _ in range(nseq)]
    if chained:
        prev_l = [next(it) for _ in range(nseq)]

    t = pl.program_id(1)
    nt = pl.num_programs(1)
    rows = ss * NSTREAM
    halo0 = CONV_HALO - (CONV_K - 1)

    if chained:
        @pl.when(t == 0)
        def _():
            for prev_s in prev_l:
                prev_s[...] = jnp.zeros(prev_s.shape, F32)

    for sq in range(nseq):
        xp_s, gbuf, ybuf, zs_s = xp_l[sq], gbuf_l[sq], ybuf_l[sq], zs_l[sq]
        if chained:
            prev_s = prev_l[sq]
        nq = d // LANES
        if il_in:
            xp_s[...] = x_ref[sq, 0].reshape(ss, NSTREAM, d)
        else:
            for j in range(ss):
                for q in range(nq):
                    xp_s[j, :, q * LANES:(q + 1) * LANES] = x_ref[
                        sq, 0, pl.ds((j // 8 * nq + q) * 8 + j % 8, NSTREAM, stride=ss * nq), :]
        x3 = xp_s[...]
        mod = mod_ref[sq:sq + 1]
        shift = mod[:, :, 0:d]
        scale = mod[:, :, d:2 * d]
        gate = mod[:, :, 2 * d:3 * d]
        hb = _modulated_rms_norm(x3, ng_ref[...], shift, scale).reshape(rows, d).astype(BF16)

        for g0 in range(0, c, grp):
            gs = slice(g0, g0 + grp)
            a = jnp.dot(hb, win_ref[:, g0:g0 + grp], preferred_element_type=F32)
            ag = jnp.dot(hb, win_ref[:, c + g0:c + g0 + grp], preferred_element_type=F32)
            gbuf[CONV_HALO:CONV_HALO + ss, :, gs] = (a * _sigmoid(ag)).reshape(ss, NSTREAM, grp)

            if chained:
                rolled = pltpu.roll(gbuf[ss:ss + CONV_HALO, :, gs], 1, axis=1)
                sub = lax.broadcasted_iota(jnp.int32, rolled.shape, 1)
                gbuf[0:CONV_HALO, :, gs] = jnp.where(sub == 0, prev_s[:, :, gs], rolled)
                prev_s[:, :, gs] = rolled
            else:
                gbuf[halo0:CONV_HALO, :, gs] = buf_ref[0, sq, :, :, gs]

            for c0 in range(g0, g0 + grp, LANES):
                cs = slice(c0, c0 + LANES)
                for j0 in range(0, ss, jb):
                    acc = [bdw_ref[:, cs] for _ in range(jb)]
                    for k in range(CONV_K):
                        wk = wdw_ref[k, :, cs]
                        for jj in range(jb):
                            acc[jj] = acc[jj] + wk * gbuf[halo0 + j0 + jj + k, :, cs]
                    for jj in range(jb):
                        ybuf[j0 + jj, :, cs] = acc[jj]

        zs_s[...] = _silu(jnp.dot(hb, win_ref[:, 2 * c:3 * c], preferred_element_type=F32)).astype(BF16)

        jc = ss // tail_chunks
        for j0 in range(0, ss, jc):
            y = ybuf[j0:j0 + jc].reshape(jc * NSTREAM, c)
            mu = jnp.mean(y, axis=-1, keepdims=True)
            yc = y - mu
            var = jnp.mean(yc * yc, axis=-1, keepdims=True)
            y = yc * lax.rsqrt(var + EPS) * lng_ref[...] + lnb_ref[...]
            y = _silu(y) * zs_s[j0 * NSTREAM:(j0 + jc) * NSTREAM, :].astype(F32)
            out = jnp.dot(y.astype(BF16), wout_ref[...], preferred_element_type=F32)
            xnew = xp_s[j0:j0 + jc] + gate * out.reshape(jc, NSTREAM, d)
            if il_out:
                xo_ref[sq, 0, j0 * NSTREAM:(j0 + jc) * NSTREAM, :] = xnew.reshape(jc * NSTREAM, d)
            else:
                xp_s[j0:j0 + jc] = xnew
                for j in range(j0, j0 + jc):
                    for q in range(nq):
                        xo_ref[sq, 0, pl.ds((j // 8 * nq + q) * 8 + j % 8, NSTREAM, stride=ss * nq), :] = xp_s[
                            j, :, q * LANES:(q + 1) * LANES]

        if not chained:
            bufo_ref[sq] = gbuf[ss + halo0:ss + CONV_HALO]

    if chained:
        @pl.when(t == nt - 1)
        def _():
            for sq in range(nseq):
                bufo_ref[sq] = prev_l[sq][halo0:CONV_HALO]


def _to_vreg_tiles(x):
    n, rows, d = x.shape
    nq = d // LANES
    return x.reshape(n, rows // 8, 8, nq, LANES).transpose(0, 1, 3, 2, 4).reshape(n, rows * nq, LANES)


def _from_vreg_tiles(x, rows, d):
    n = x.shape[0]
    nq = d // LANES
    return x.reshape(n, rows // 8, nq, 8, LANES).transpose(0, 1, 3, 2, 4).reshape(n, rows, d)


def _conv_layer(x, mod, state, layer, norm_g, w_in, w_dw, b_dw, ln_g, ln_b, w_out, *, ss, il_in, il_out, nseq):
    b, t, d = x.shape
    c = w_dw.shape[1]
    k1 = CONV_K - 1
    rows = NSTREAM * ss
    nq = d // LANES
    chained = state is None
    assert ss >= CONV_HALO and ss % 8 == 0
    if chained:
        assert t % rows == 0 and b % nseq == 0
        nblk, nt = b, t // rows
        mod3 = mod.reshape(b, 1, 3 * d)
        mod_spec = pl.BlockSpec((nseq, 1, 3 * d), lambda i, j: (i, 0, 0))
    else:
        assert t == ss and b % NSTREAM == 0 and nseq == 1
        nblk, nt = b // NSTREAM, 1
        mod3 = mod.reshape(nblk, NSTREAM, 3 * d)
        mod_spec = pl.BlockSpec((1, NSTREAM, 3 * d), lambda i, j: (i, 0, 0))
    xt = x.reshape(nblk, nt, rows, d)
    tile_spec = pl.BlockSpec((nseq, 1, rows, d), lambda i, j: (i, j, 0, 0))
    vreg_tile_spec = pl.BlockSpec((nseq, 1, rows * nq, LANES), lambda i, j: (i, j, 0, 0))
    kern = functools.partial(_conv_layer_kernel, ss=ss, d=d, c=c, jb=8, grp=256, tail_chunks=2, chained=chained,
                             il_in=il_in, il_out=il_out, nseq=nseq)
    inputs = [xt if il_in else _to_vreg_tiles(xt.reshape(nblk * nt, rows, d)).reshape(nblk, nt, rows * nq, LANES),
              mod3]
    in_specs = [tile_spec if il_in else vreg_tile_spec, mod_spec]
    if not chained:
        n_layers = state.shape[0]
        st = state.reshape(n_layers, nblk, NSTREAM, k1, c).transpose(0, 1, 3, 2, 4)
        inputs.append(st)
        in_specs.append(pl.BlockSpec((1, 1, k1, NSTREAM, c), lambda i, j: (layer, i, 0, 0, 0)))
    w_rep = jnp.broadcast_to(w_dw[:, None, :], (CONV_K, NSTREAM, c))
    b_rep = jnp.broadcast_to(b_dw[None, :], (NSTREAM, c))
    inputs += [norm_g.reshape(1, d), w_in, w_rep, b_rep, ln_g.reshape(1, c), ln_b.reshape(1, c), w_out]
    in_specs += [
        _const_spec((1, d)),
        _const_spec((d, 3 * c)),
        _const_spec((CONV_K, NSTREAM, c)),
        _const_spec((NSTREAM, c)),
        _const_spec((1, c)),
        _const_spec((1, c)),
        _const_spec((c, d)),
    ]
    scratch = ([pltpu.VMEM((ss, NSTREAM, d), F32) for _ in range(nseq)]
               + [pltpu.VMEM((CONV_HALO + ss, NSTREAM, c), F32) for _ in range(nseq)]
               + [pltpu.VMEM((ss, NSTREAM, c), F32) for _ in range(nseq)]
               + [pltpu.VMEM((rows, c), BF16) for _ in range(nseq)])
    if chained:
        scratch += [pltpu.VMEM((CONV_HALO, NSTREAM, c), F32) for _ in range(nseq)]
    xo_shape = (nblk, nt, rows, d) if il_out else (nblk, nt, rows * nq, LANES)
    xo, bufo = pl.pallas_call(
        kern,
        out_shape=(jax.ShapeDtypeStruct(xo_shape, F32), jax.ShapeDtypeStruct((nblk, k1, NSTREAM, c), F32)),
        grid=(nblk // nseq, nt),
        in_specs=in_specs,
        out_specs=(
            tile_spec if il_out else vreg_tile_spec,
            pl.BlockSpec((nseq, k1, NSTREAM, c), lambda i, j: (i, 0, 0, 0)),
        ),
        scratch_shapes=scratch,
        compiler_params=pltpu.CompilerParams(dimension_semantics=("arbitrary", "arbitrary"),
                                             vmem_limit_bytes=VMEM_LIMIT_BYTES),
        name="conv_layer",
    )(*inputs)
    if chained:
        new_buf = bufo[:, :, 0, :]
    else:
        new_buf = bufo.transpose(0, 2, 1, 3).reshape(b, k1, c)
    if not il_out:
        xo = _from_vreg_tiles(xo.reshape(nblk * nt, rows * nq, LANES), rows, d)
    return xo.reshape(b, t, d), new_buf


def _mlstm_layer_kernel(*refs, ll, d, w, dh, has_state, aliased, interleaved, gate_split):
    it = iter(refs)
    x_ref, mod_ref = next(it), next(it)
    if has_state:
        mbuf_ref, c0_ref, n0_ref, m0_ref = next(it), next(it), next(it), next(it)
    (ng_ref, win_ref, wc_ref, bc_ref, wq_ref, wk_ref, wv_ref, wg_ref, bg_ref, gng_ref, skip_ref,
     wout_ref) = [next(it) for _ in range(12)]
    if aliased:
        next(it)
    xo_ref, mbufo_ref, co_ref, no_ref, mo_ref = [next(it) for _ in range(5)]
    xbuf, xc_s, q_s, k_s, v_s, zs_s, og_s, hh_s = [next(it) for _ in range(8)]
    if interleaved:
        prev_s = next(it)

    t = pl.program_id(1)
    nt = pl.num_programs(1)
    nh = M_HEADS

    @pl.when(t == 0)
    def _():
        if has_state:
            xbuf[MCONV_HALO - (M_CONV_K - 1):MCONV_HALO, :] = mbuf_ref[0, 0]
            co_ref[...] = c0_ref[...]
            no_ref[...] = n0_ref[0]
            mo_ref[...] = m0_ref[0]
        else:
            if interleaved:
                prev_s[...] = jnp.zeros(prev_s.shape, F32)
            else:
                xbuf[0:MCONV_HALO, :] = jnp.zeros((MCONV_HALO, w), F32)
            co_ref[...] = jnp.zeros(co_ref.shape, F32)
            no_ref[...] = jnp.zeros(no_ref.shape, F32)
            mo_ref[...] = jnp.zeros(mo_ref.shape, F32)

    x = x_ref[0]
    mod = mod_ref[0]
    shift = mod[:, 0:d]
    scale = mod[:, d:2 * d]
    gate = mod[:, 2 * d:3 * d]
    hb = _modulated_rms_norm(x, ng_ref[...], shift, scale).astype(BF16)
    xm = jnp.dot(hb, win_ref[:, 0:w], preferred_element_type=F32)
    zs_s[...] = _silu(jnp.dot(hb, win_ref[:, w:2 * w], preferred_element_type=F32)).astype(BF16)
    og_s[...] = _sigmoid(jnp.dot(hb, win_ref[:, 2 * w:3 * w], preferred_element_type=F32)).astype(BF16)

    base = MCONV_HALO - (M_CONV_K - 1)
    if interleaved:
        ssm = ll // NSTREAM
        xbuf[MCONV_HALO:MCONV_HALO + ssm] = xm.reshape(ssm, NSTREAM, w)
        rolled = pltpu.roll(xbuf[ssm:ssm + MCONV_HALO], 1, axis=1)
        sub = lax.broadcasted_iota(jnp.int32, rolled.shape, 1)
        xbuf[0:MCONV_HALO] = jnp.where(sub == 0, prev_s[...], rolled)
        prev_s[...] = rolled
        xc = jnp.broadcast_to(bc_ref[...][None], (ssm, NSTREAM, w))
        for k in range(M_CONV_K):
            xc = xc + wc_ref[k:k + 1, :][None] * xbuf[base + k:base + k + ssm]
        xc = _silu(xc).reshape(ll, w)
    else:
        xbuf[MCONV_HALO:MCONV_HALO + ll, :] = xm
        xc = jnp.broadcast_to(bc_ref[...], (ll, w))
        for k in range(M_CONV_K):
            xc = xc + wc_ref[k:k + 1, :] * xbuf[base + k:base + k + ll, :]
        xc = _silu(xc)
    xc_s[...] = xc

    xc_b = xc.astype(BF16)
    xm_b = xm.astype(BF16)
    for hd in range(nh):
        sl = slice(hd * dh, (hd + 1) * dh)
        q_s[:, sl] = jnp.dot(xc_b[:, sl], wq_ref[hd], preferred_element_type=F32).astype(BF16)
        k_s[:, sl] = jnp.dot(xc_b[:, sl], wk_ref[hd], preferred_element_type=F32).astype(BF16)
        v_s[:, sl] = jnp.dot(xm_b[:, sl], wv_ref[hd], preferred_element_type=F32).astype(BF16)

    gpre = bg_ref[...]
    for op, src_s in enumerate((q_s, k_s, v_s)):
        if gate_split:
            hw = w // 2
            lo = jnp.dot(src_s[:, 0:hw], wg_ref[op], preferred_element_type=F32)
            hi = jnp.dot(src_s[:, hw:w], wg_ref[op], preferred_element_type=F32)
            gpre = gpre + lo[:, 0:GATE_PAD] + hi[:, GATE_PAD:2 * GATE_PAD]
        else:
            gpre = gpre + jnp.dot(src_s[...], wg_ref[op], preferred_element_type=F32)
    lf = _log_sigmoid(gpre)
    row = lax.broadcasted_iota(jnp.int32, (ll, ll), 0)
    col = lax.broadcasted_iota(jnp.int32, (ll, ll), 1)
    if interleaved:
        shift_bits = NSTREAM.bit_length() - 1
        row = jnp.bitwise_and(row, NSTREAM - 1) * ssm + jnp.right_shift(row, shift_bits)
        col = jnp.bitwise_and(col, NSTREAM - 1) * ssm + jnp.right_shift(col, shift_bits)
    causal = col <= row
    nt_dims = (((1,), (1,)), ((), ()))
    if gate_split:
        tril = jnp.where(causal, 1.0, 0.0).astype(BF16)
        lf_hi = lf.astype(BF16)
        r1 = lf - lf_hi.astype(F32)
        lf_mid = r1.astype(BF16)
        lf_lo = (r1 - lf_mid.astype(F32)).astype(BF16)
        bcum = (jnp.dot(tril, lf_hi, preferred_element_type=F32) + jnp.dot(tril, lf_mid, preferred_element_type=F32)
                + jnp.dot(tril, lf_lo, preferred_element_type=F32))
        gpre_t = gpre.T
        bcum_t = bcum.T
    else:
        tril = jnp.where(causal, 1.0, 0.0).astype(F32)
        bcum = jnp.dot(tril, lf, preferred_element_type=F32, precision=HIGHEST)
        ident = jnp.where(lax.broadcasted_iota(jnp.int32, (GATE_PAD, GATE_PAD), 0)
                          == lax.broadcasted_iota(jnp.int32, (GATE_PAD, GATE_PAD), 1), 1.0, 0.0).astype(F32)
        gpre_t = lax.dot_general(ident, gpre, nt_dims, preferred_element_type=F32, precision=HIGHEST)
        bcum_t = lax.dot_general(ident, bcum, nt_dims, preferred_element_type=F32, precision=HIGHEST)

    kscale = dh ** -0.5
    for hd in range(nh):
        sl = slice(hd * dh, (hd + 1) * dh)
        ig_c = gpre[:, hd:hd + 1]
        b_c = bcum[:, nh + hd:nh + hd + 1]
        ig_r = gpre_t[hd:hd + 1, :]
        b_r = bcum_t[nh + hd:nh + hd + 1, :]
        m_prev = mo_ref[0, :, hd:hd + 1]
        c_prev = co_ref[0, 0, hd]
        n_prev = no_ref[0, hd:hd + 1, :]
        qb = q_s[:, sl]
        kb = k_s[:, sl]
        vb = v_s[:, sl]

        a = b_c + m_prev
        logw = jnp.where(causal, b_c - b_r + ig_r, -jnp.inf)
        mt = jnp.maximum(a, jnp.max(logw, axis=-1, keepdims=True))
        w_inter = jnp.exp(a - mt)
        w_intra = jnp.exp(logw - mt) * kscale
        s = lax.dot_general(qb, kb, nt_dims, preferred_element_type=F32) * w_intra
        inter = lax.dot_general(qb, c_prev.astype(BF16), nt_dims, preferred_element_type=F32)
        num = w_inter * inter + jnp.dot(s.astype(BF16), vb, preferred_element_type=F32)
        qn = jnp.sum(qb.astype(F32) * n_prev, axis=-1, keepdims=True)
        den = w_inter * qn + jnp.sum(s, axis=-1, keepdims=True)
        den = jnp.maximum(jnp.abs(den), jnp.exp(-mt))
        hh_s[:, sl] = num * (1.0 / den)

        m_new = mt[ll - 1:ll, :]
        b_last = b_c[ll - 1:ll, :]
        decay = jnp.exp(b_last + m_prev - m_new)
        wk_c = jnp.exp(b_last - b_c + ig_c - m_new) * kscale
        vw = (vb.astype(F32) * wk_c).astype(BF16)
        upd = lax.dot_general(vw, kb, (((0,), (0,)), ((), ())), preferred_element_type=F32)
        co_ref[0, 0, hd] = decay * c_prev + upd
        no_ref[0, hd:hd + 1, :] = decay * n_prev + jnp.sum(kb.astype(F32) * wk_c, axis=0, keepdims=True)
        mo_ref[0, :, hd:hd + 1] = m_new

    for hd in range(nh):
        sl = slice(hd * dh, (hd + 1) * dh)
        hv = og_s[:, sl].astype(F32) * hh_s[:, sl]
        mu = jnp.mean(hv, axis=-1, keepdims=True)
        hc = hv - mu
        var = jnp.mean(hc * hc, axis=-1, keepdims=True)
        hn = hc * lax.rsqrt(var + EPS) * gng_ref[:, sl]
        hh_s[:, sl] = (hn + skip_ref[:, sl] * xc_s[:, sl]) * zs_s[:, sl].astype(F32)
    out = jnp.dot(hh_s[...].astype(BF16), wout_ref[...], preferred_element_type=F32)
    xo_ref[0] = x + gate * out

    if interleaved:
        @pl.when(t == nt - 1)
        def _():
            mbufo_ref[0] = prev_s[base:MCONV_HALO]
    else:
        @pl.when(t == nt - 1)
        def _():
            mbufo_ref[0] = xbuf[ll + base:ll + MCONV_HALO, :]

        @pl.when(t < nt - 1)
        def _():
            xbuf[0:MCONV_HALO, :] = xbuf[ll:ll + MCONV_HALO, :]


def _mlstm_layer(x, mod, state, c_stack, layer, n_layers, norm_g, w_in, w_conv, b_conv, w_q, w_k, w_v, w_gate_pad,
                 b_gate_pad, gn_g, skip, w_out, *, ll, interleaved):
    b, t, d = x.shape
    w = w_conv.shape[1]
    nh = M_HEADS
    dh = w // nh
    assert t % ll == 0 and ll % 8 == 0 and ll >= MCONV_HALO
    has_state = state is not None
    aliased = c_stack is not None
    km = M_CONV_K - 1
    assert not (interleaved and has_state) and (not interleaved or ll // NSTREAM >= MCONV_HALO)
    gate_split = ll % LANES == 0
    if gate_split:
        w_gate_pad = jnp.concatenate([w_gate_pad[:, :w // 2], w_gate_pad[:, w // 2:]], axis=-1)
    kern = functools.partial(_mlstm_layer_kernel, ll=ll, d=d, w=w, dh=dh, has_state=has_state, aliased=aliased,
                             interleaved=interleaved, gate_split=gate_split)
    inputs = [x, mod.reshape(b, 1, 3 * d)]
    in_specs = [
        pl.BlockSpec((1, ll, d), lambda i, j: (i, j, 0)),
        pl.BlockSpec((1, 1, 3 * d), lambda i, j: (i, 0, 0)),
    ]
    if has_state:
        mbuf, c0, n0, m0 = state
        inputs += [mbuf, c0, n0, m0.reshape(m0.shape[0], b, 1, nh)]
        in_specs += [
            pl.BlockSpec((1, 1, km, w), lambda i, j: (layer, i, 0, 0)),
            pl.BlockSpec((1, 1, nh, dh, dh), lambda i, j: (layer, i, 0, 0, 0)),
            pl.BlockSpec((1, 1, nh, dh), lambda i, j: (layer, i, 0, 0)),
            pl.BlockSpec((1, 1, 1, nh), lambda i, j: (layer, i, 0, 0)),
        ]
    inputs += [norm_g.reshape(1, d), w_in, w_conv, b_conv.reshape(1, w), w_q, w_k, w_v, w_gate_pad, b_gate_pad,
               gn_g.reshape(1, w), skip.reshape(1, w), w_out]
    in_specs += [
        _const_spec((1, d)),
        _const_spec((d, 3 * w)),
        _const_spec((M_CONV_K, w)),
        _const_spec((1, w)),
        _const_spec((nh, dh, dh)),
        _const_spec((nh, dh, dh)),
        _const_spec((nh, dh, dh)),
        _const_spec(tuple(w_gate_pad.shape)),
        _const_spec((1, GATE_PAD)),
        _const_spec((1, w)),
        _const_spec((1, w)),
        _const_spec((w, d)),
    ]
    aliases = {}
    if aliased:
        aliases = {len(inputs): 2}
        inputs.append(c_stack)
        in_specs.append(pl.BlockSpec(memory_space=pl.ANY))
    if interleaved:
        mbuf_shape, mbuf_spec = (b, km, NSTREAM, w), pl.BlockSpec((1, km, NSTREAM, w), lambda i, j: (i, 0, 0, 0))
        conv_scratch = [pltpu.VMEM((MCONV_HALO + ll // NSTREAM, NSTREAM, w), F32)]
    else:
        mbuf_shape, mbuf_spec = (b, km, w), pl.BlockSpec((1, km, w), lambda i, j: (i, 0, 0))
        conv_scratch = [pltpu.VMEM((MCONV_HALO + ll, w), F32)]
    scratch = conv_scratch + [
        pltpu.VMEM((ll, w), F32),
        pltpu.VMEM((ll, w), BF16),
        pltpu.VMEM((ll, w), BF16),
        pltpu.VMEM((ll, w), BF16),
        pltpu.VMEM((ll, w), BF16),
        pltpu.VMEM((ll, w), BF16),
        pltpu.VMEM((ll, w), F32),
    ]
    if interleaved:
        scratch.append(pltpu.VMEM((MCONV_HALO, NSTREAM, w), F32))
    xo, nb, c_new, n_new, m_new = pl.pallas_call(
        kern,
        out_shape=(
            jax.ShapeDtypeStruct((b, t, d), F32),
            jax.ShapeDtypeStruct(mbuf_shape, F32),
            jax.ShapeDtypeStruct((n_layers, b, nh, dh, dh), F32),
            jax.ShapeDtypeStruct((b, nh, dh), F32),
            jax.ShapeDtypeStruct((b, 1, nh), F32),
        ),
        grid=(b, t // ll),
        in_specs=in_specs,
        out_specs=(
            pl.BlockSpec((1, ll, d), lambda i, j: (i, j, 0)),
            mbuf_spec,
            pl.BlockSpec((1, 1, nh, dh, dh), lambda i, j: (layer, i, 0, 0, 0)),
            pl.BlockSpec((1, nh, dh), lambda i, j: (i, 0, 0)),
            pl.BlockSpec((1, 1, nh), lambda i, j: (i, 0, 0)),
        ),
        scratch_shapes=scratch,
        input_output_aliases=aliases,
        compiler_params=pltpu.CompilerParams(dimension_semantics=("arbitrary", "arbitrary"),
                                             vmem_limit_bytes=VMEM_LIMIT_BYTES),
        name="mlstm_layer",
    )(*inputs)
    if interleaved:
        nb = nb[:, :, 0, :]
    return xo, nb, c_new, n_new, m_new.reshape(b, nh)


def _final_norm_kernel(x_ref, g_ref, o_ref, *scratch, tile_rows, d, interleaved):
    if interleaved:
        (xs,) = scratch
        nq = d // LANES
        ssn = tile_rows // NSTREAM
        for u in range(xs.shape[0] // tile_rows):
            for s in range(NSTREAM):
                for j0 in range(0, ssn, 8):
                    for q in range(nq):
                        f0 = u * tile_rows + s * ssn + j0
                        xs[f0:f0 + 8, q * LANES:(q + 1) * LANES] = x_ref[
                            0, pl.ds(((u * ssn + j0) * nq + q) * NSTREAM + s, 8, stride=nq * NSTREAM), :]
        x = xs[...]
    else:
        x = x_ref[...]
    o_ref[...] = x * lax.rsqrt(jnp.mean(x * x, axis=-1, keepdims=True) + EPS) * g_ref[...]


def _final_norm(x, g, *, tile_rows, rows_blk=1024):
    b, t, d = x.shape
    rows = b * t
    rows_blk = min(rows_blk, rows)
    interleaved = tile_rows is not None
    kern = functools.partial(_final_norm_kernel, tile_rows=tile_rows, d=d, interleaved=interleaved)
    if interleaved:
        nq = d // LANES
        xin = _to_vreg_tiles(x.reshape(rows // rows_blk, rows_blk, d))
        x_spec = pl.BlockSpec((1, rows_blk * nq, LANES), lambda i: (i, 0, 0))
        scratch = [pltpu.VMEM((rows_blk, d), F32)]
    else:
        xin = x.reshape(rows, d)
        x_spec = pl.BlockSpec((rows_blk, d), lambda i: (i, 0))
        scratch = []
    y = pl.pallas_call(
        kern,
        out_shape=jax.ShapeDtypeStruct((rows, d), F32),
        grid=(rows // rows_blk,),
        in_specs=[x_spec, pl.BlockSpec((1, d), lambda i: (0, 0))],
        out_specs=pl.BlockSpec((rows_blk, d), lambda i: (i, 0)),
        scratch_shapes=scratch,
        compiler_params=pltpu.CompilerParams(dimension_semantics=("arbitrary",)),
        name="final_norm",
    )(xin, g.reshape(1, d))
    return y.reshape(b, t, d)


def _prep_weights(P):
    n_ml, _, nh, dh, ng = P['ml_w_gate'].shape
    wg = P['ml_w_gate'].reshape(n_ml, 3, nh * dh, ng)
    wg = jnp.pad(wg, ((0, 0), (0, 0), (0, 0), (0, GATE_PAD - ng))).astype(BF16)
    bg = jnp.pad(P['ml_b_gate'], ((0, 0), (0, GATE_PAD - ng))).reshape(n_ml, 1, GATE_PAD)
    return dict(
        cv_w_in=P['cv_w_in'].astype(BF16), cv_w_out=P['cv_w_out'].astype(BF16),
        ml_w_in=P['ml_w_in'].astype(BF16), ml_w_q=P['ml_w_q'].astype(BF16), ml_w_k=P['ml_w_k'].astype(BF16),
        ml_w_v=P['ml_w_v'].astype(BF16), ml_w_out=P['ml_w_out'].astype(BF16), ml_w_gate=wg, ml_b_gate=bg)


def _run_trunk(x, mods, state, P, W, *, conv_ss, ml_ll):
    depth = P['norm_g'].shape[0]
    n_ml = P['ml_w_in'].shape[0]
    il = state is None
    if il:
        assert conv_ss * NSTREAM == ml_ll
    new_conv, new_mconv, new_n, new_m = [], [], [], []
    c_stack = None
    for i in range(depth):
        j = i // 2
        if i % 2 == 0:
            x, nb = _conv_layer(x, mods[i], None if state is None else state[0], j, P['norm_g'][i], W['cv_w_in'][j],
                                P['cv_w_dw'][j], P['cv_b_dw'][j], P['cv_ln_g'][j], P['cv_ln_b'][j], W['cv_w_out'][j],
                                ss=conv_ss, il_in=il and i > 0, il_out=il, nseq=CONV_NSEQ if il else 1)
            new_conv.append(nb)
        else:
            x, nb, c_stack, n, m = _mlstm_layer(
                x, mods[i], None if state is None else state[1:], c_stack, j, n_ml, P['norm_g'][i], W['ml_w_in'][j],
                P['ml_w_conv'][j], P['ml_b_conv'][j], W['ml_w_q'][j], W['ml_w_k'][j], W['ml_w_v'][j],
                W['ml_w_gate'][j], W['ml_b_gate'][j], P['ml_gn_g'][j], P['ml_skip'][j], W['ml_w_out'][j],
                ll=ml_ll, interleaved=il)
            new_mconv.append(nb)
            new_n.append(n)
            new_m.append(m)
    y = _final_norm(x, P['final_g'], tile_rows=ml_ll if il else None)
    return y, jnp.stack(new_conv), jnp.stack(new_mconv), c_stack, jnp.stack(new_n), jnp.stack(new_m)


def kernel(x_prompt, x_sample, c_prompt, c_sample, state_conv, state_mconv, state_C, state_n, state_m, norm_g, ada_w, ada_b, cv_w_in, cv_w_dw, cv_b_dw, cv_ln_g, cv_ln_b, cv_w_out, ml_w_in, ml_w_conv, ml_b_conv, ml_w_q, ml_w_k, ml_w_v, ml_w_gate, ml_b_gate, ml_gn_g, ml_skip, ml_w_out, final_g):
    P = dict(norm_g=norm_g, cv_w_in=cv_w_in, cv_w_dw=cv_w_dw, cv_b_dw=cv_b_dw, cv_ln_g=cv_ln_g, cv_ln_b=cv_ln_b,
             cv_w_out=cv_w_out, ml_w_in=ml_w_in, ml_w_conv=ml_w_conv, ml_b_conv=ml_b_conv, ml_w_q=ml_w_q,
             ml_w_k=ml_w_k, ml_w_v=ml_w_v, ml_w_gate=ml_w_gate, ml_b_gate=ml_b_gate, ml_gn_g=ml_gn_g,
             ml_skip=ml_skip, ml_w_out=ml_w_out, final_g=final_g)
    W = _prep_weights(P)
    bp, tp, _ = x_prompt.shape
    bs, ts, _ = x_sample.shape

    mods = _ada_call(jnp.concatenate([c_prompt, c_sample], axis=0), ada_w, ada_b)
    mods_p = mods[:, :bp]
    mods_s = mods[:, bp:]

    outs_p = _run_trunk(x_prompt, mods_p, None, P, W, conv_ss=PROMPT_TILE // NSTREAM, ml_ll=PROMPT_TILE)
    outs_s = _run_trunk(x_sample, mods_s, (state_conv, state_mconv, state_C, state_n, state_m), P, W,
                        conv_ss=ts, ml_ll=ts)
    return (outs_p[0], outs_s[0]) + tuple(outs_p[1:]) + tuple(outs_s[1:])
```

```python
import functools

import jax
import jax.numpy as jnp
from jax import lax
from jax.experimental import pallas as pl
from jax.experimental.pallas import tpu as pltpu

EPS = 1e-6
CONV_K = 31
M_CONV_K = 4
M_HEADS = 4
GATE_PAD = 128
CONV_HALO = 32
MCONV_HALO = 8
NSTREAM = 8
LANES = 128
CONV_NSEQ = 2
PROMPT_TILE = 256
VMEM_LIMIT_BYTES = 58 * 1024 * 1024

F32 = jnp.float32
BF16 = jnp.bfloat16
HIGHEST = lax.Precision.HIGHEST


def _const_spec(shape):
    nd = len(shape)
    return pl.BlockSpec(shape, lambda *_: (0,) * nd, pipeline_mode=pl.Buffered(1))


def _sigmoid(x):
    return 1.0 / (1.0 + jnp.exp(-x))


def _silu(x):
    return x * _sigmoid(x)


def _log_sigmoid(x):
    return jnp.minimum(x, 0.0) - jnp.log1p(jnp.exp(-jnp.abs(x)))


def _modulated_rms_norm(x, g, shift, scale):
    y = x * lax.rsqrt(jnp.mean(x * x, axis=-1, keepdims=True) + EPS)
    return (y * g) * (1.0 + scale) + shift


def _ada_kernel(c_ref, w_ref, b_ref, o_ref):
    o_ref[0] = jnp.dot(c_ref[...], w_ref[0], preferred_element_type=F32, precision=HIGHEST) + b_ref[0]


def _ada_call(c_all, ada_w, ada_b):
    depth, d, e = ada_w.shape
    nb = c_all.shape[0]
    tn = 1024
    return pl.pallas_call(
        _ada_kernel,
        out_shape=jax.ShapeDtypeStruct((depth, nb, e), F32),
        grid=(depth, e // tn),
        in_specs=[
            pl.BlockSpec((nb, d), lambda i, j: (0, 0)),
            pl.BlockSpec((1, d, tn), lambda i, j: (i, 0, j)),
            pl.BlockSpec((1, 1, tn), lambda i, j: (i, 0, j)),
        ],
        out_specs=pl.BlockSpec((1, nb, tn), lambda i, j: (i, 0, j)),
        compiler_params=pltpu.CompilerParams(dimension_semantics=("arbitrary", "arbitrary")),
        name="ada_mod",
    )(c_all, ada_w, ada_b.reshape(depth, 1, e))


def _conv_layer_kernel(*refs, ss, d, c, jb, grp, tail_chunks, chained, il_in, il_out, nseq):
    it = iter(refs)
    x_ref, mod_ref = next(it), next(it)
    if not chained:
        buf_ref = next(it)
    ng_ref, win_ref, wdw_ref, bdw_ref, lng_ref, lnb_ref, wout_ref = [next(it) for _ in range(7)]
    xo_ref, bufo_ref = next(it), next(it)
    xp_l = [next(it) for _ in range(nseq)]
    gbuf_l = [next(it) for _ in range(nseq)]
    ybuf_l = [next(it) for _ in range(nseq)]
    zs_l = [next(it) for _ in range(nseq)]
    if chained:
        prev_l = [next(it) for _ in range(nseq)]

    t = pl.program_id(1)
    nt = pl.num_programs(1)
    rows = ss * NSTREAM
    halo0 = CONV_HALO - (CONV_K - 1)

    if chained:
        @pl.when(t == 0)
        def _():
            for prev_s in prev_l:
                prev_s[...] = jnp.zeros(prev_s.shape, F32)

    for sq in range(nseq):
        xp_s, gbuf, ybuf, zs_s = xp_l[sq], gbuf_l[sq], ybuf_l[sq], zs_l[sq]
        if chained:
            prev_s = prev_l[sq]
        nq = d // LANES
        if il_in:
            xp_s[...] = x_ref[sq, 0].reshape(ss, NSTREAM, d)
        else:
            for j in range(ss):
                for q in range(nq):
                    xp_s[j, :, q * LANES:(q + 1) * LANES] = x_ref[
                        sq, 0, pl.ds((j // 8 * nq + q) * 8 + j % 8, NSTREAM, stride=ss * nq), :]
        x3 = xp_s[...]
        mod = mod_ref[sq:sq + 1]
        shift = mod[:, :, 0:d]
        scale = mod[:, :, d:2 * d]
        gate = mod[:, :, 2 * d:3 * d]
        hb = _modulated_rms_norm(x3, ng_ref[...], shift, scale).reshape(rows, d).astype(BF16)

        for g0 in range(0, c, grp):
            gs = slice(g0, g0 + grp)
            a = jnp.dot(hb, win_ref[:, g0:g0 + grp], preferred_element_type=F32)
            ag = jnp.dot(hb, win_ref[:, c + g0:c + g0 + grp], preferred_element_type=F32)
            gbuf[CONV_HALO:CONV_HALO + ss, :, gs] = (a * _sigmoid(ag)).reshape(ss, NSTREAM, grp)

            if chained:
                rolled = pltpu.roll(gbuf[ss:ss + CONV_HALO, :, gs], 1, axis=1)
                sub = lax.broadcasted_iota(jnp.int32, rolled.shape, 1)
                gbuf[0:CONV_HALO, :, gs] = jnp.where(sub == 0, prev_s[:, :, gs], rolled)
                prev_s[:, :, gs] = rolled
            else:
                gbuf[halo0:CONV_HALO, :, gs] = buf_ref[0, sq, :, :, gs]

            for c0 in range(g0, g0 + grp, LANES):
                cs = slice(c0, c0 + LANES)
                for j0 in range(0, ss, jb):
                    acc = [bdw_ref[:, cs] for _ in range(jb)]
                    for k in range(CONV_K):
                        wk = wdw_ref[k, :, cs]
                        for jj in range(jb):
                            acc[jj] = acc[jj] + wk * gbuf[halo0 + j0 + jj + k, :, cs]
                    for jj in range(jb):
                        ybuf[j0 + jj, :, cs] = acc[jj]

        zs_s[...] = _silu(jnp.dot(hb, win_ref[:, 2 * c:3 * c], preferred_element_type=F32)).astype(BF16)

        jc = ss // tail_chunks
        for j0 in range(0, ss, jc):
            y = ybuf[j0:j0 + jc].reshape(jc * NSTREAM, c)
            mu = jnp.mean(y, axis=-1, keepdims=True)
            yc = y - mu
            var = jnp.mean(yc * yc, axis=-1, keepdims=True)
            y = yc * lax.rsqrt(var + EPS) * lng_ref[...] + lnb_ref[...]
            y = _silu(y) * zs_s[j0 * NSTREAM:(j0 + jc) * NSTREAM, :].astype(F32)
            out = jnp.dot(y.astype(BF16), wout_ref[...], preferred_element_type=F32)
            xnew = xp_s[j0:j0 + jc] + gate * out.reshape(jc, NSTREAM, d)
            if il_out:
                xo_ref[sq, 0, j0 * NSTREAM:(j0 + jc) * NSTREAM, :] = xnew.reshape(jc * NSTREAM, d)
            else:
                xp_s[j0:j0 + jc] = xnew
                for j in range(j0, j0 + jc):
                    for q in range(nq):
                        xo_ref[sq, 0, pl.ds((j // 8 * nq + q) * 8 + j % 8, NSTREAM, stride=ss * nq), :] = xp_s[
                            j, :, q * LANES:(q + 1) * LANES]

        if not chained:
            bufo_ref[sq] = gbuf[ss + halo0:ss + CONV_HALO]

    if chained:
        @pl.when(t == nt - 1)
        def _():
            for sq in range(nseq):
                bufo_ref[sq] = prev_l[sq][halo0:CONV_HALO]


def _to_vreg_tiles(x):
    n, rows, d = x.shape
    nq = d // LANES
    return x.reshape(n, rows // 8, 8, nq, LANES).transpose(0, 1, 3, 2, 4).reshape(n, rows * nq, LANES)


def _from_vreg_tiles(x, rows, d):
    n = x.shape[0]
    nq = d // LANES
    return x.reshape(n, rows // 8, nq, 8, LANES).transpose(0, 1, 3, 2, 4).reshape(n, rows, d)


def _conv_layer(x, mod, state, layer, norm_g, w_in, w_dw, b_dw, ln_g, ln_b, w_out, *, ss, il_in, il_out, nseq):
    b, t, d = x.shape
    c = w_dw.shape[1]
    k1 = CONV_K - 1
    rows = NSTREAM * ss
    nq = d // LANES
    chained = state is None
    assert ss >= CONV_HALO and ss % 8 == 0
    if chained:
        assert t % rows == 0 and b % nseq == 0
        nblk, nt = b, t // rows
        mod3 = mod.reshape(b, 1, 3 * d)
        mod_spec = pl.BlockSpec((nseq, 1, 3 * d), lambda i, j: (i, 0, 0))
    else:
        assert t == ss and b % NSTREAM == 0 and nseq == 1
        nblk, nt = b // NSTREAM, 1
        mod3 = mod.reshape(nblk, NSTREAM, 3 * d)
        mod_spec = pl.BlockSpec((1, NSTREAM, 3 * d), lambda i, j: (i, 0, 0))
    xt = x.reshape(nblk, nt, rows, d)
    tile_spec = pl.BlockSpec((nseq, 1, rows, d), lambda i, j: (i, j, 0, 0))
    vreg_tile_spec = pl.BlockSpec((nseq, 1, rows * nq, LANES), lambda i, j: (i, j, 0, 0))
    kern = functools.partial(_conv_layer_kernel, ss=ss, d=d, c=c, jb=8, grp=256, tail_chunks=2, chained=chained,
                             il_in=il_in, il_out=il_out, nseq=nseq)
    inputs = [xt if il_in else _to_vreg_tiles(xt.reshape(nblk * nt, rows, d)).reshape(nblk, nt, rows * nq, LANES),
              mod3]
    in_specs = [tile_spec if il_in else vreg_tile_spec, mod_spec]
    if not chained:
        n_layers = state.shape[0]
        st = state.reshape(n_layers, nblk, NSTREAM, k1, c).transpose(0, 1, 3, 2, 4)
        inputs.append(st)
        in_specs.append(pl.BlockSpec((1, 1, k1, NSTREAM, c), lambda i, j: (layer, i, 0, 0, 0)))
    w_rep = jnp.broadcast_to(w_dw[:, None, :], (CONV_K, NSTREAM, c))
    b_rep = jnp.broadcast_to(b_dw[None, :], (NSTREAM, c))
    inputs += [norm_g.reshape(1, d), w_in, w_rep, b_rep, ln_g.reshape(1, c), ln_b.reshape(1, c), w_out]
    in_specs += [
        _const_spec((1, d)),
        _const_spec((d, 3 * c)),
        _const_spec((CONV_K, NSTREAM, c)),
        _const_spec((NSTREAM, c)),
        _const_spec((1, c)),
        _const_spec((1, c)),
        _const_spec((c, d)),
    ]
    scratch = ([pltpu.VMEM((ss, NSTREAM, d), F32) for _ in range(nseq)]
               + [pltpu.VMEM((CONV_HALO + ss, NSTREAM, c), F32) for _ in range(nseq)]
               + [pltpu.VMEM((ss, NSTREAM, c), F32) for _ in range(nseq)]
               + [pltpu.VMEM((rows, c), BF16) for _ in range(nseq)])
    if chained:
        scratch += [pltpu.VMEM((CONV_HALO, NSTREAM, c), F32) for _ in range(nseq)]
    xo_shape = (nblk, nt, rows, d) if il_out else (nblk, nt, rows * nq, LANES)
    xo, bufo = pl.pallas_call(
        kern,
        out_shape=(jax.ShapeDtypeStruct(xo_shape, F32), jax.ShapeDtypeStruct((nblk, k1, NSTREAM, c), F32)),
        grid=(nblk // nseq, nt),
        in_specs=in_specs,
        out_specs=(
            tile_spec if il_out else vreg_tile_spec,
            pl.BlockSpec((nseq, k1, NSTREAM, c), lambda i, j: (i, 0, 0, 0)),
        ),
        scratch_shapes=scratch,
        compiler_params=pltpu.CompilerParams(dimension_semantics=("arbitrary", "arbitrary"),
                                             vmem_limit_bytes=VMEM_LIMIT_BYTES),
        name="conv_layer",
    )(*inputs)
    if chained:
        new_buf = bufo[:, :, 0, :]
    else:
        new_buf = bufo.transpose(0, 2, 1, 3).reshape(b, k1, c)
    if not il_out:
        xo = _from_vreg_tiles(xo.reshape(nblk * nt, rows * nq, LANES), rows, d)
    return xo.reshape(b, t, d), new_buf


def _mlstm_layer_kernel(*refs, ll, d, w, dh, has_state, aliased, interleaved, gate_split):
    it = iter(refs)
    x_ref, mod_ref = next(it), next(it)
    if has_state:
        mbuf_ref, c0_ref, n0_ref, m0_ref = next(it), next(it), next(it), next(it)
    (ng_ref, win_ref, wc_ref, bc_ref, wq_ref, wk_ref, wv_ref, wg_ref, bg_ref, gng_ref, skip_ref,
     wout_ref) = [next(it) for _ in range(12)]
    if aliased:
        next(it)
    xo_ref, mbufo_ref, co_ref, no_ref, mo_ref = [next(it) for _ in range(5)]
    xbuf, xc_s, q_s, k_s, v_s, zs_s, og_s, hh_s = [next(it) for _ in range(8)]
    if interleaved:
        prev_s = next(it)

    t = pl.program_id(1)
    nt = pl.num_programs(1)
    nh = M_HEADS

    @pl.when(t == 0)
    def _():
        if has_state:
            xbuf[MCONV_HALO - (M_CONV_K - 1):MCONV_HALO, :] = mbuf_ref[0, 0]
            co_ref[...] = c0_ref[...]
            no_ref[...] = n0_ref[0]
            mo_ref[...] = m0_ref[0]
        else:
            if interleaved:
                prev_s[...] = jnp.zeros(prev_s.shape, F32)
            else:
                xbuf[0:MCONV_HALO, :] = jnp.zeros((MCONV_HALO, w), F32)
            co_ref[...] = jnp.zeros(co_ref.shape, F32)
            no_ref[...] = jnp.zeros(no_ref.shape, F32)
            mo_ref[...] = jnp.zeros(mo_ref.shape, F32)

    x = x_ref[0]
    mod = mod_ref[0]
    shift = mod[:, 0:d]
    scale = mod[:, d:2 * d]
    gate = mod[:, 2 * d:3 * d]
    hb = _modulated_rms_norm(x, ng_ref[...], shift, scale).astype(BF16)
    xm = jnp.dot(hb, win_ref[:, 0:w], preferred_element_type=F32)
    zs_s[...] = _silu(jnp.dot(hb, win_ref[:, w:2 * w], preferred_element_type=F32)).astype(BF16)
    og_s[...] = _sigmoid(jnp.dot(hb, win_ref[:, 2 * w:3 * w], preferred_element_type=F32)).astype(BF16)

    base = MCONV_HALO - (M_CONV_K - 1)
    if interleaved:
        ssm = ll // NSTREAM
        xbuf[MCONV_HALO:MCONV_HALO + ssm] = xm.reshape(ssm, NSTREAM, w)
        rolled = pltpu.roll(xbuf[ssm:ssm + MCONV_HALO], 1, axis=1)
        sub = lax.broadcasted_iota(jnp.int32, rolled.shape, 1)
        xbuf[0:MCONV_HALO] = jnp.where(sub == 0, prev_s[...], rolled)
        prev_s[...] = rolled
        xc = jnp.broadcast_to(bc_ref[...][None], (ssm, NSTREAM, w))
        for k in range(M_CONV_K):
            xc = xc + wc_ref[k:k + 1, :][None] * xbuf[base + k:base + k + ssm]
        xc = _silu(xc).reshape(ll, w)
    else:
        xbuf[MCONV_HALO:MCONV_HALO + ll, :] = xm
        xc = jnp.broadcast_to(bc_ref[...], (ll, w))
        for k in range(M_CONV_K):
            xc = xc + wc_ref[k:k + 1, :] * xbuf[base + k:base + k + ll, :]
        xc = _silu(xc)
    xc_s[...] = xc

    xc_b = xc.astype(BF16)
    xm_b = xm.astype(BF16)
    for hd in range(nh):
        sl = slice(hd * dh, (hd + 1) * dh)
        q_s[:, sl] = jnp.dot(xc_b[:, sl], wq_ref[hd], preferred_element_type=F32).astype(BF16)
        k_s[:, sl] = jnp.dot(xc_b[:, sl], wk_ref[hd], preferred_element_type=F32).astype(BF16)
        v_s[:, sl] = jnp.dot(xm_b[:, sl], wv_ref[hd], preferred_element_type=F32).astype(BF16)

    gpre = bg_ref[...]
    for op, src_s in enumerate((q_s, k_s, v_s)):
        if gate_split:
            hw = w // 2
            lo = jnp.dot(src_s[:, 0:hw], wg_ref[op], preferred_element_type=F32)
            hi = jnp.dot(src_s[:, hw:w], wg_ref[op], preferred_element_type=F32)
            gpre = gpre + lo[:, 0:GATE_PAD] + hi[:, GATE_PAD:2 * GATE_PAD]
        else:
            gpre = gpre + jnp.dot(src_s[...], wg_ref[op], preferred_element_type=F32)
    lf = _log_sigmoid(gpre)
    row = lax.broadcasted_iota(jnp.int32, (ll, ll), 0)
    col = lax.broadcasted_iota(jnp.int32, (ll, ll), 1)
    if interleaved:
        shift_bits = NSTREAM.bit_length() - 1
        row = jnp.bitwise_and(row, NSTREAM - 1) * ssm + jnp.right_shift(row, shift_bits)
        col = jnp.bitwise_and(col, NSTREAM - 1) * ssm + jnp.right_shift(col, shift_bits)
    causal = col <= row
    nt_dims = (((1,), (1,)), ((), ()))
    if gate_split:
        tril = jnp.where(causal, 1.0, 0.0).astype(BF16)
        lf_hi = lf.astype(BF16)
        r1 = lf - lf_hi.astype(F32)
        lf_mid = r1.astype(BF16)
        lf_lo = (r1 - lf_mid.astype(F32)).astype(BF16)
        bcum = (jnp.dot(tril, lf_hi, preferred_element_type=F32) + jnp.dot(tril, lf_mid, preferred_element_type=F32)
                + jnp.dot(tril, lf_lo, preferred_element_type=F32))
        gpre_t = gpre.T
        bcum_t = bcum.T
    else:
        tril = jnp.where(causal, 1.0, 0.0).astype(F32)
        bcum = jnp.dot(tril, lf, preferred_element_type=F32, precision=HIGHEST)
        ident = jnp.where(lax.broadcasted_iota(jnp.int32, (GATE_PAD, GATE_PAD), 0)
                          == lax.broadcasted_iota(jnp.int32, (GATE_PAD, GATE_PAD), 1), 1.0, 0.0).astype(F32)
        gpre_t = lax.dot_general(ident, gpre, nt_dims, preferred_element_type=F32, precision=HIGHEST)
        bcum_t = lax.dot_general(ident, bcum, nt_dims, preferred_element_type=F32, precision=HIGHEST)

    kscale = dh ** -0.5
    for hd in range(nh):
        sl = slice(hd * dh, (hd + 1) * dh)
        ig_c = gpre[:, hd:hd + 1]
        b_c = bcum[:, nh + hd:nh + hd + 1]
        ig_r = gpre_t[hd:hd + 1, :]
        b_r = bcum_t[nh + hd:nh + hd + 1, :]
        m_prev = mo_ref[0, :, hd:hd + 1]
        c_prev = co_ref[0, 0, hd]
        n_prev = no_ref[0, hd:hd + 1, :]
        qb = q_s[:, sl]
        kb = k_s[:, sl]
        vb = v_s[:, sl]

        a = b_c + m_prev
        logw = jnp.where(causal, b_c - b_r + ig_r, -jnp.inf)
        mt = jnp.maximum(a, jnp.max(logw, axis=-1, keepdims=True))
        w_inter = jnp.exp(a - mt)
        w_intra = jnp.exp(logw - mt) * kscale
        s = lax.dot_general(qb, kb, nt_dims, preferred_element_type=F32) * w_intra
        inter = lax.dot_general(qb, c_prev.astype(BF16), nt_dims, preferred_element_type=F32)
        num = w_inter * inter + jnp.dot(s.astype(BF16), vb, preferred_element_type=F32)
        qn = jnp.sum(qb.astype(F32) * n_prev, axis=-1, keepdims=True)
        den = w_inter * qn + jnp.sum(s, axis=-1, keepdims=True)
        den = jnp.maximum(jnp.abs(den), jnp.exp(-mt))
        hh_s[:, sl] = num * (1.0 / den)

        m_new = mt[ll - 1:ll, :]
        b_last = b_c[ll - 1:ll, :]
        decay = jnp.exp(b_last + m_prev - m_new)
        wk_c = jnp.exp(b_last - b_c + ig_c - m_new) * kscale
        vw = (vb.astype(F32) * wk_c).astype(BF16)
        upd = lax.dot_general(vw, kb, (((0,), (0,)), ((), ())), preferred_element_type=F32)
        co_ref[0, 0, hd] = decay * c_prev + upd
        no_ref[0, hd:hd + 1, :] = decay * n_prev + jnp.sum(kb.astype(F32) * wk_c, axis=0, keepdims=True)
        mo_ref[0, :, hd:hd + 1] = m_new

    for hd in range(nh):
        sl = slice(hd * dh, (hd + 1) * dh)
        hv = og_s[:, sl].astype(F32) * hh_s[:, sl]
        mu = jnp.mean(hv, axis=-1, keepdims=True)
        hc = hv - mu
        var = jnp.mean(hc * hc, axis=-1, keepdims=True)
        hn = hc * lax.rsqrt(var + EPS) * gng_ref[:, sl]
        hh_s[:, sl] = (hn + skip_ref[:, sl] * xc_s[:, sl]) * zs_s[:, sl].astype(F32)
    out = jnp.dot(hh_s[...].astype(BF16), wout_ref[...], preferred_element_type=F32)
    xo_ref[0] = x + gate * out

    if interleaved:
        @pl.when(t == nt - 1)
        def _():
            mbufo_ref[0] = prev_s[base:MCONV_HALO]
    else:
        @pl.when(t == nt - 1)
        def _():
            mbufo_ref[0] = xbuf[ll + base:ll + MCONV_HALO, :]

        @pl.when(t < nt - 1)
        def _():
            xbuf[0:MCONV_HALO, :] = xbuf[ll:ll + MCONV_HALO, :]


def _mlstm_layer(x, mod, state, c_stack, layer, n_layers, norm_g, w_in, w_conv, b_conv, w_q, w_k, w_v, w_gate_pad,
                 b_gate_pad, gn_g, skip, w_out, *, ll, interleaved):
    b, t, d = x.shape
    w = w_conv.shape[1]
    nh = M_HEADS
    dh = w // nh
    assert t % ll == 0 and ll % 8 == 0 and ll >= MCONV_HALO
    has_state = state is not None
    aliased = c_stack is not None
    km = M_CONV_K - 1
    assert not (interleaved and has_state) and (not interleaved or ll // NSTREAM >= MCONV_HALO)
    gate_split = ll % LANES == 0
    if gate_split:
        w_gate_pad = jnp.concatenate([w_gate_pad[:, :w // 2], w_gate_pad[:, w // 2:]], axis=-1)
    kern = functools.partial(_mlstm_layer_kernel, ll=ll, d=d, w=w, dh=dh, has_state=has_state, aliased=aliased,
                             interleaved=interleaved, gate_split=gate_split)
    inputs = [x, mod.reshape(b, 1, 3 * d)]
    in_specs = [
        pl.BlockSpec((1, ll, d), lambda i, j: (i, j, 0)),
        pl.BlockSpec((1, 1, 3 * d), lambda i, j: (i, 0, 0)),
    ]
    if has_state:
        mbuf, c0, n0, m0 = state
        inputs += [mbuf, c0, n0, m0.reshape(m0.shape[0], b, 1, nh)]
        in_specs += [
            pl.BlockSpec((1, 1, km, w), lambda i, j: (layer, i, 0, 0)),
            pl.BlockSpec((1, 1, nh, dh, dh), lambda i, j: (layer, i, 0, 0, 0)),
            pl.BlockSpec((1, 1, nh, dh), lambda i, j: (layer, i, 0, 0)),
            pl.BlockSpec((1, 1, 1, nh), lambda i, j: (layer, i, 0, 0)),
        ]
    inputs += [norm_g.reshape(1, d), w_in, w_conv, b_conv.reshape(1, w), w_q, w_k, w_v, w_gate_pad, b_gate_pad,
               gn_g.reshape(1, w), skip.reshape(1, w), w_out]
    in_specs += [
        _const_spec((1, d)),
        _const_spec((d, 3 * w)),
        _const_spec((M_CONV_K, w)),
        _const_spec((1, w)),
        _const_spec((nh, dh, dh)),
        _const_spec((nh, dh, dh)),
        _const_spec((nh, dh, dh)),
        _const_spec(tuple(w_gate_pad.shape)),
        _const_spec((1, GATE_PAD)),
        _const_spec((1, w)),
        _const_spec((1, w)),
        _const_spec((w, d)),
    ]
    aliases = {}
    if aliased:
        aliases = {len(inputs): 2}
        inputs.append(c_stack)
        in_specs.append(pl.BlockSpec(memory_space=pl.ANY))
    if interleaved:
        mbuf_shape, mbuf_spec = (b, km, NSTREAM, w), pl.BlockSpec((1, km, NSTREAM, w), lambda i, j: (i, 0, 0, 0))
        conv_scratch = [pltpu.VMEM((MCONV_HALO + ll // NSTREAM, NSTREAM, w), F32)]
    else:
        mbuf_shape, mbuf_spec = (b, km, w), pl.BlockSpec((1, km, w), lambda i, j: (i, 0, 0))
        conv_scratch = [pltpu.VMEM((MCONV_HALO + ll, w), F32)]
    scratch = conv_scratch + [
        pltpu.VMEM((ll, w), F32),
        pltpu.VMEM((ll, w), BF16),
        pltpu.VMEM((ll, w), BF16),
        pltpu.VMEM((ll, w), BF16),
        pltpu.VMEM((ll, w), BF16),
        pltpu.VMEM((ll, w), BF16),
        pltpu.VMEM((ll, w), F32),
    ]
    if interleaved:
        scratch.append(pltpu.VMEM((MCONV_HALO, NSTREAM, w), F32))
    xo, nb, c_new, n_new, m_new = pl.pallas_call(
        kern,
        out_shape=(
            jax.ShapeDtypeStruct((b, t, d), F32),
            jax.ShapeDtypeStruct(mbuf_shape, F32),
            jax.ShapeDtypeStruct((n_layers, b, nh, dh, dh), F32),
            jax.ShapeDtypeStruct((b, nh, dh), F32),
            jax.ShapeDtypeStruct((b, 1, nh), F32),
        ),
        grid=(b, t // ll),
        in_specs=in_specs,
        out_specs=(
            pl.BlockSpec((1, ll, d), lambda i, j: (i, j, 0)),
            mbuf_spec,
            pl.BlockSpec((1, 1, nh, dh, dh), lambda i, j: (layer, i, 0, 0, 0)),
            pl.BlockSpec((1, nh, dh), lambda i, j: (i, 0, 0)),
            pl.BlockSpec((1, 1, nh), lambda i, j: (i, 0, 0)),
        ),
        scratch_shapes=scratch,
        input_output_aliases=aliases,
        compiler_params=pltpu.CompilerParams(dimension_semantics=("arbitrary", "arbitrary"),
                                             vmem_limit_bytes=VMEM_LIMIT_BYTES),
        name="mlstm_layer",
    )(*inputs)
    if interleaved:
        nb = nb[:, :, 0, :]
    return xo, nb, c_new, n_new, m_new.reshape(b, nh)


def _final_norm_kernel(x_ref, g_ref, o_ref, *scratch, tile_rows, d, interleaved):
    if interleaved:
        (xs,) = scratch
        nq = d // LANES
        ssn = tile_rows // NSTREAM
        for u in range(xs.shape[0] // tile_rows):
            for s in range(NSTREAM):
                for j0 in range(0, ssn, 8):
                    for q in range(nq):
                        f0 = u * tile_rows + s * ssn + j0
                        xs[f0:f0 + 8, q * LANES:(q + 1) * LANES] = x_ref[
                            0, pl.ds(((u * ssn + j0) * nq + q) * NSTREAM + s, 8, stride=nq * NSTREAM), :]
        x = xs[...]
    else:
        x = x_ref[...]
    o_ref[...] = x * lax.rsqrt(jnp.mean(x * x, axis=-1, keepdims=True) + EPS) * g_ref[...]


def _final_norm(x, g, *, tile_rows, rows_blk=1024):
    b, t, d = x.shape
    rows = b * t
    rows_blk = min(rows_blk, rows)
    interleaved = tile_rows is not None
    kern = functools.partial(_final_norm_kernel, tile_rows=tile_rows, d=d, interleaved=interleaved)
    if interleaved:
        nq = d // LANES
        xin = _to_vreg_tiles(x.reshape(rows // rows_blk, rows_blk, d))
        x_spec = pl.BlockSpec((1, rows_blk * nq, LANES), lambda i: (i, 0, 0))
        scratch = [pltpu.VMEM((rows_blk, d), F32)]
    else:
        xin = x.reshape(rows, d)
        x_spec = pl.BlockSpec((rows_blk, d), lambda i: (i, 0))
        scratch = []
    y = pl.pallas_call(
        kern,
        out_shape=jax.ShapeDtypeStruct((rows, d), F32),
        grid=(rows // rows_blk,),
        in_specs=[x_spec, pl.BlockSpec((1, d), lambda i: (0, 0))],
        out_specs=pl.BlockSpec((rows_blk, d), lambda i: (i, 0)),
        scratch_shapes=scratch,
        compiler_params=pltpu.CompilerParams(dimension_semantics=("arbitrary",)),
        name="final_norm",
    )(xin, g.reshape(1, d))
    return y.reshape(b, t, d)


def _prep_weights(P):
    n_ml, _, nh, dh, ng = P['ml_w_gate'].shape
    wg = P['ml_w_gate'].reshape(n_ml, 3, nh * dh, ng)
    wg = jnp.pad(wg, ((0, 0), (0, 0), (0, 0), (0, GATE_PAD - ng))).astype(BF16)
    bg = jnp.pad(P['ml_b_gate'], ((0, 0), (0, GATE_PAD - ng))).reshape(n_ml, 1, GATE_PAD)
    return dict(
        cv_w_in=P['cv_w_in'].astype(BF16), cv_w_out=P['cv_w_out'].astype(BF16),
        ml_w_in=P['ml_w_in'].astype(BF16), ml_w_q=P['ml_w_q'].astype(BF16), ml_w_k=P['ml_w_k'].astype(BF16),
        ml_w_v=P['ml_w_v'].astype(BF16), ml_w_out=P['ml_w_out'].astype(BF16), ml_w_gate=wg, ml_b_gate=bg)


def _run_trunk(x, mods, state, P, W, *, conv_ss, ml_ll):
    depth = P['norm_g'].shape[0]
    n_ml = P['ml_w_in'].shape[0]
    il = state is None
    if il:
        assert conv_ss * NSTREAM == ml_ll
    new_conv, new_mconv, new_n, new_m = [], [], [], []
    _, nh, dh, _ = P['ml_w_q'].shape
    c_stack = jnp.zeros((n_ml, x.shape[0], nh, dh, dh), F32)
    for i in range(depth):
        j = i // 2
        if i % 2 == 0:
            x, nb = _conv_layer(x, mods[i], None if state is None else state[0], j, P['norm_g'][i], W['cv_w_in'][j],
                                P['cv_w_dw'][j], P['cv_b_dw'][j], P['cv_ln_g'][j], P['cv_ln_b'][j], W['cv_w_out'][j],
                                ss=conv_ss, il_in=il and i > 0, il_out=il, nseq=CONV_NSEQ if il else 1)
            new_conv.append(nb)
        else:
            x, nb, c_stack, n, m = _mlstm_layer(
                x, mods[i], None if state is None else state[1:], c_stack, j, n_ml, P['norm_g'][i], W['ml_w_in'][j],
                P['ml_w_conv'][j], P['ml_b_conv'][j], W['ml_w_q'][j], W['ml_w_k'][j], W['ml_w_v'][j],
                W['ml_w_gate'][j], W['ml_b_gate'][j], P['ml_gn_g'][j], P['ml_skip'][j], W['ml_w_out'][j],
                ll=ml_ll, interleaved=il)
            new_mconv.append(nb)
            new_n.append(n)
            new_m.append(m)
    y = _final_norm(x, P['final_g'], tile_rows=ml_ll if il else None)
    return y, jnp.stack(new_conv), jnp.stack(new_mconv), c_stack, jnp.stack(new_n), jnp.stack(new_m)


def kernel(x_prompt, x_sample, c_prompt, c_sample, state_conv, state_mconv, state_C, state_n, state_m, norm_g, ada_w, ada_b, cv_w_in, cv_w_dw, cv_b_dw, cv_ln_g, cv_ln_b, cv_w_out, ml_w_in, ml_w_conv, ml_b_conv, ml_w_q, ml_w_k, ml_w_v, ml_w_gate, ml_b_gate, ml_gn_g, ml_skip, ml_w_out, final_g):
    P = dict(norm_g=norm_g, cv_w_in=cv_w_in, cv_w_dw=cv_w_dw, cv_b_dw=cv_b_dw, cv_ln_g=cv_ln_g, cv_ln_b=cv_ln_b,
             cv_w_out=cv_w_out, ml_w_in=ml_w_in, ml_w_conv=ml_w_conv, ml_b_conv=ml_b_conv, ml_w_q=ml_w_q,
             ml_w_k=ml_w_k, ml_w_v=ml_w_v, ml_w_gate=ml_w_gate, ml_b_gate=ml_b_gate, ml_gn_g=ml_gn_g,
             ml_skip=ml_skip, ml_w_out=ml_w_out, final_g=final_g)
    W = _prep_weights(P)
    bp, tp, _ = x_prompt.shape
    bs, ts, _ = x_sample.shape

    mods = _ada_call(jnp.concatenate([c_prompt, c_sample], axis=0), ada_w, ada_b)
    mods_p = mods[:, :bp]
    mods_s = mods[:, bp:]

    outs_p = _run_trunk(x_prompt, mods_p, None, P, W, conv_ss=PROMPT_TILE // NSTREAM, ml_ll=PROMPT_TILE)
    outs_s = _run_trunk(x_sample, mods_s, (state_conv, state_mconv, state_C, state_n, state_m), P, W,
                        conv_ss=ts, ml_ll=ts)
    return (outs_p[0], outs_s[0]) + tuple(outs_p[1:]) + tuple(outs_s[1:])
```
